```python
import math
import jax, jax.numpy as jnp
from jax import lax
import numpy as np

D_MODEL = 1024
BATCH = 1
SEQ = 16384
DEPTH = 4
DEC_BATCH = 16
DEC_SEQ = 4096
PAST_LEN = 128

ATTN_GROUPS = ((128, 1), (512, 4), (2048, 16))
N_GROUPS = 3
HEADS_PER_GROUP = 8
HEAD_DIM = 64
N_ATTN_HEADS = N_GROUPS * HEADS_PER_GROUP
ATTN_QK_WIDTH = N_ATTN_HEADS * HEAD_DIM
ATTN_V_WIDTH = HEADS_PER_GROUP * HEAD_DIM
M_WIDTH = D_MODEL
M_HEADS = 4
M_HEAD_DIM = M_WIDTH // M_HEADS
M_CHUNK = 128
CONV_WIDTH = 5
FFN_HIDDEN = int(math.ceil(8 * D_MODEL / 3 / 256)) * 256
IN_SIZES = (ATTN_QK_WIDTH, ATTN_QK_WIDTH, ATTN_V_WIDTH, 2 * M_WIDTH, M_WIDTH, M_WIDTH, 4 * M_HEADS, 2 * D_MODEL)
IN_SPLITS = tuple(int(c) for c in np.cumsum(IN_SIZES)[:-1])
N_IN = int(sum(IN_SIZES))
EPS = 1e-6

kernel_name = "hybrid_dilated_attn_mlstm_encoder"


def alibi_slopes():
    h = np.arange(1, N_ATTN_HEADS + 1, dtype=np.float32)
    return (2.0 ** (-8.0 * h / N_ATTN_HEADS)).astype(np.float32).reshape(N_GROUPS, HEADS_PER_GROUP)


def rms_norm(x, g):
    xf = x.astype(jnp.float32)
    y = xf * lax.rsqrt(jnp.mean(xf * xf, axis=-1, keepdims=True) + EPS)
    return (y * g.astype(jnp.float32)).astype(x.dtype)


def dilated_window_attention(q, k, v, slopes, dilation, half_win):
    b, h, s, hd = q.shape
    sr = s // dilation
    blk = half_win
    nb = -(-sr // blk)
    extra = nb * blk - sr

    def to_residue(t):
        return t.reshape(b, h, sr, dilation, hd).swapaxes(2, 3)

    def windows(t):
        tp = jnp.pad(t, ((0, 0), (0, 0), (0, 0), (half_win, half_win + extra), (0, 0)))
        tp = tp.reshape(b, h, dilation, nb + 2, blk, hd)
        return jnp.concatenate([tp[:, :, :, :-2], tp[:, :, :, 1:-1], tp[:, :, :, 2:]], axis=-2)

    qb = jnp.pad(to_residue(q), ((0, 0), (0, 0), (0, 0), (0, extra), (0, 0))).reshape(b, h, dilation, nb, blk, hd)
    kw = windows(to_residue(k))
    vw = windows(to_residue(v))
    scores = jnp.einsum('bhcnqd,bhcnkd->bhcnqk', qb, kw) * (hd ** -0.5)
    qi = jnp.arange(blk)[:, None]
    ki = jnp.arange(3 * blk)[None, :]
    step = ki - half_win - qi
    key_idx = jnp.arange(nb)[:, None, None] * blk - half_win + ki[None]
    valid = (jnp.abs(step) <= half_win)[None] & (key_idx >= 0) & (key_idx < sr)
    alibi = -jnp.asarray(slopes)[:, None, None] * (jnp.abs(step) * dilation).astype(jnp.float32)[None]
    scores = jnp.where(valid, scores + alibi[:, None, None], -jnp.inf)
    lse = jax.nn.logsumexp(scores, axis=-1)
    out = jnp.einsum('bhcnqk,bhcnkd->bhcnqd', jnp.exp(scores - lse[..., None]), vw)
    out = out.reshape(b, h, dilation, nb * blk, hd)[:, :, :, :sr]
    lse = lse.reshape(b, h, dilation, nb * blk)[:, :, :, :sr]
    return out.swapaxes(2, 3).reshape(b, h, s, hd), lse.swapaxes(2, 3).reshape(b, h, s)


def attention_branch(aq, ak, av):
    b, s = aq.shape[0], aq.shape[1]
    slopes = alibi_slopes()
    v = av.astype(jnp.float32).transpose(0, 2, 1, 3)
    outs, lses = [], []
    for g, (window, dilation) in enumerate(ATTN_GROUPS):
        q = aq[:, :, g].astype(jnp.float32).transpose(0, 2, 1, 3)
        k = ak[:, :, g].astype(jnp.float32).transpose(0, 2, 1, 3)
        o, l = dilated_window_attention(q, k, v, slopes[g], dilation, window // (2 * dilation))
        outs.append(o)
        lses.append(l)
    wts = jax.nn.softmax(jnp.stack(lses, axis=0), axis=0)
    out = jnp.einsum('gbhs,gbhsd->bhsd', wts, jnp.stack(outs, axis=0))
    return out.transpose(0, 2, 1, 3).reshape(b, s, ATTN_V_WIDTH).astype(aq.dtype)


def mlstm_direction(q, k, v, log_i, log_f):
    b, h, s, d = q.shape
    nc = s // M_CHUNK

    def chunks(t):
        return jnp.moveaxis(t.reshape(b, h, nc, M_CHUNK, *t.shape[3:]), 2, 0)

    lower = jnp.tril(jnp.ones((M_CHUNK, M_CHUNK), dtype=bool))

    def step(carry, xs):
        c_prev, n_prev, m_prev = carry
        qc, kc, vc, li, lf = xs
        cum = jnp.cumsum(lf, axis=-1)
        dmat = jnp.where(lower, cum[..., :, None] - cum[..., None, :] + li[..., None, :], -jnp.inf)
        m_inter = cum + m_prev[..., None]
        m_t = jnp.maximum(m_inter, jnp.max(dmat, axis=-1))
        w_inter = jnp.exp(m_inter - m_t)
        sc = jnp.einsum('bhtd,bhsd->bhts', qc, kc) * jnp.exp(dmat - m_t[..., None])
        num = jnp.einsum('bhts,bhsd->bhtd', sc, vc) + w_inter[..., None] * jnp.einsum('bhtk,bhkv->bhtv', qc, c_prev)
        den = jnp.sum(sc, axis=-1) + w_inter * jnp.einsum('bhtk,bhk->bht', qc, n_prev)
        h_out = num / jnp.maximum(jnp.abs(den), jnp.exp(-m_t))[..., None]
        tot = cum[..., -1]
        a = tot[..., None] - cum + li
        m_new = jnp.maximum(tot + m_prev, jnp.max(a, axis=-1))
        w_k = jnp.exp(a - m_new[..., None])
        decay = jnp.exp(tot + m_prev - m_new)
        c_new = decay[..., None, None] * c_prev + jnp.einsum('bhs,bhsk,bhsv->bhkv', w_k, kc, vc)
        n_new = decay[..., None] * n_prev + jnp.einsum('bhs,bhsk->bhk', w_k, kc)
        return (c_new, n_new, m_new), h_out

    init = (jnp.zeros((b, h, d, d), jnp.float32), jnp.zeros((b, h, d), jnp.float32), jnp.zeros((b, h), jnp.float32))
    _, hs = lax.scan(step, init, (chunks(q), chunks(k), chunks(v), chunks(log_i), chunks(log_f)))
    return jnp.moveaxis(hs, 0, 2).reshape(b, h, s, d)


def mlstm_branch(mq, mk, mv, mo, mg, g_norm):
    b, s, _ = mq.shape

    def heads(t):
        return t.astype(jnp.float32).reshape(b, s, M_HEADS, M_HEAD_DIM).transpose(0, 2, 1, 3)

    qh = heads(mq) * (M_HEAD_DIM ** -0.5)
    kh = heads(mk)
    vh = heads(mv)
    g = mg.astype(jnp.float32).reshape(b, s, 4, M_HEADS).transpose(2, 0, 3, 1)
    h_fwd = mlstm_direction(qh, kh, vh, g[0], jax.nn.log_sigmoid(g[1]))
    rev = lambda t: jnp.flip(t, axis=2)
    h_bwd = rev(mlstm_direction(rev(qh), rev(kh), rev(vh), rev(g[2]), rev(jax.nn.log_sigmoid(g[3]))))
    hsum = h_fwd + h_bwd
    hsum = hsum * lax.rsqrt(jnp.mean(hsum * hsum, axis=-1, keepdims=True) + EPS)
    hsum = hsum.transpose(0, 2, 1, 3).reshape(b, s, M_WIDTH) * g_norm.astype(jnp.float32)
    return (jax.nn.sigmoid(mo.astype(jnp.float32)) * hsum).astype(mq.dtype)


def centred_depthwise_conv(x, w, bias):
    y = lax.conv_general_dilated(x, w[:, None, :], window_strides=(1,),
                                 padding=[(CONV_WIDTH // 2, CONV_WIDTH // 2)],
                                 dimension_numbers=('NWC', 'WIO', 'NWC'),
                                 feature_group_count=x.shape[-1])
    return y + bias


def encoder_trunk(x, norm_mix_pre, norm_mix_post, norm_ffn_pre, norm_ffn_post, w_in, b_mlstm_gates,
                  w_conv, b_conv, g_mlstm_norm, w_attn_proj, w_mlstm_proj, w_out, w_ffn_in, w_ffn_out):
    b, s, _ = x.shape
    for l in range(DEPTH):
        xn = rms_norm(x, norm_mix_pre[l])
        proj = xn @ w_in[l]
        aq, ak, av, mqk, mv, mo, mg, bg = jnp.split(proj, IN_SPLITS, axis=-1)
        aq = aq.reshape(b, s, N_GROUPS, HEADS_PER_GROUP, HEAD_DIM)
        ak = ak.reshape(b, s, N_GROUPS, HEADS_PER_GROUP, HEAD_DIM)
        av = av.reshape(b, s, HEADS_PER_GROUP, HEAD_DIM)
        attn_out = attention_branch(aq, ak, av)
        mqk = jax.nn.silu(centred_depthwise_conv(mqk, w_conv[l], b_conv[l]))
        mq, mk = jnp.split(mqk, 2, axis=-1)
        mlstm_out = mlstm_branch(mq, mk, mv, mo, mg + b_mlstm_gates[l], g_mlstm_norm[l])
        gate_a, gate_m = jnp.split(jax.nn.sigmoid(bg), 2, axis=-1)
        merged = gate_a * (attn_out @ w_attn_proj[l]) + gate_m * (mlstm_out @ w_mlstm_proj[l])
        x = x + rms_norm(merged @ w_out[l], norm_mix_post[l])
        xn = rms_norm(x, norm_ffn_pre[l])
        gate, up = jnp.split(xn @ w_ffn_in[l], 2, axis=-1)
        x = x + rms_norm((jax.nn.silu(gate) * up) @ w_ffn_out[l], norm_ffn_post[l])
    return x


def setup_inputs(seed: int = 0) -> dict:
    key = jax.random.key(seed)
    ks = jax.random.split(key, 18)
    nrm = jax.random.normal
    gain = lambda k, n: 1.0 + 0.02 * nrm(k, (DEPTH, n), jnp.float32)
    fbias = jax.random.uniform(ks[7], (DEPTH, 4, M_HEADS), jnp.float32, minval=3.0, maxval=6.0)
    ibias = 0.1 * nrm(ks[8], (DEPTH, 4, M_HEADS), jnp.float32)
    is_forget = jnp.array([False, True, False, True])[None, :, None]
    return {
        'x_prompt': nrm(ks[0], (BATCH, SEQ, D_MODEL), jnp.float32),
        'x_sample': nrm(ks[1], (DEC_BATCH, DEC_SEQ, D_MODEL), jnp.float32),
        'norm_mix_pre': gain(ks[2], D_MODEL),
        'norm_mix_post': gain(ks[3], D_MODEL),
        'norm_ffn_pre': gain(ks[4], D_MODEL),
        'norm_ffn_post': gain(ks[5], D_MODEL),
        'w_in': nrm(ks[6], (DEPTH, D_MODEL, N_IN), jnp.float32) * D_MODEL ** -0.5,
        'b_mlstm_gates': jnp.where(is_forget, fbias, ibias).reshape(DEPTH, 4 * M_HEADS),
        'w_conv': nrm(ks[9], (DEPTH, CONV_WIDTH, 2 * M_WIDTH), jnp.float32) * CONV_WIDTH ** -0.5,
        'b_conv': 0.02 * nrm(ks[10], (DEPTH, 2 * M_WIDTH), jnp.float32),
        'g_mlstm_norm': gain(ks[11], M_WIDTH),
        'w_attn_proj': nrm(ks[12], (DEPTH, ATTN_V_WIDTH, D_MODEL), jnp.float32) * ATTN_V_WIDTH ** -0.5,
        'w_mlstm_proj': nrm(ks[13], (DEPTH, M_WIDTH, D_MODEL), jnp.float32) * M_WIDTH ** -0.5,
        'w_out': nrm(ks[14], (DEPTH, D_MODEL, D_MODEL), jnp.float32) * D_MODEL ** -0.5,
        'w_ffn_in': nrm(ks[15], (DEPTH, D_MODEL, 2 * FFN_HIDDEN), jnp.float32) * D_MODEL ** -0.5,
        'w_ffn_out': nrm(ks[16], (DEPTH, FFN_HIDDEN, D_MODEL), jnp.float32) * FFN_HIDDEN ** -0.5,
    }


def reference(x_prompt, x_sample, norm_mix_pre, norm_mix_post, norm_ffn_pre, norm_ffn_post, w_in, b_mlstm_gates,
              w_conv, b_conv, g_mlstm_norm, w_attn_proj, w_mlstm_proj, w_out, w_ffn_in, w_ffn_out):
    y_prompt = encoder_trunk(x_prompt, norm_mix_pre, norm_mix_post, norm_ffn_pre, norm_ffn_post, w_in, b_mlstm_gates,
                             w_conv, b_conv, g_mlstm_norm, w_attn_proj, w_mlstm_proj, w_out, w_ffn_in, w_ffn_out)
    y_sample = encoder_trunk(x_sample, norm_mix_pre, norm_mix_post, norm_ffn_pre, norm_ffn_post, w_in, b_mlstm_gates,
                             w_conv, b_conv, g_mlstm_norm, w_attn_proj, w_mlstm_proj, w_out, w_ffn_in, w_ffn_out)
    return (y_prompt, y_sample)
```

```python
import functools
import math

import numpy as np
import jax
import jax.numpy as jnp
from jax import lax
from jax.experimental import pallas as pl
from jax.experimental.pallas import tpu as pltpu

F32 = jnp.float32
BF16 = jnp.bfloat16

D_MODEL = 1024
DEPTH = 4
ATTN_GROUPS = ((128, 1), (512, 4), (2048, 16))
N_GROUPS = 3
HEADS_PER_GROUP = 8
HEAD_DIM = 64
HALF_WIN = 64
ATTN_W = HEADS_PER_GROUP * HEAD_DIM
M_WIDTH = 1024
M_HEADS = 4
M_HEAD_DIM = 256
M_CHUNK = 128
CONV_WIDTH = 5
FFN_HIDDEN = 2816
EPS = 1e-6
NEG = -1e30

_SIZES = (3 * ATTN_W, 3 * ATTN_W, ATTN_W, 2 * M_WIDTH, M_WIDTH, M_WIDTH, 4 * M_HEADS, 2 * D_MODEL)
_OFFS = tuple(int(c) for c in np.cumsum((0,) + _SIZES))
N_MAIN = sum(_SIZES) - 4 * M_HEADS

VMEM_LIMIT = 56 * 1024 * 1024


def _alibi_slopes():
    h = np.arange(1, N_GROUPS * HEADS_PER_GROUP + 1, dtype=np.float32)
    return (2.0 ** (-8.0 * h / (N_GROUPS * HEADS_PER_GROUP))).astype(np.float32).reshape(N_GROUPS, HEADS_PER_GROUP)


def _rms(x, gain):
    return x * lax.rsqrt(jnp.mean(x * x, axis=-1, keepdims=True) + EPS) * gain


def _resident(shape):
    nd = len(shape)
    return pl.BlockSpec(shape, lambda *_: (0,) * nd, pipeline_mode=pl.Buffered(1))


def _params(*sem):
    return pltpu.CompilerParams(dimension_semantics=sem, vmem_limit_bytes=VMEM_LIMIT)


def _inproj_kernel(x_ref, gain_ref, w_ref, wg_ref, bias_ref,
                   q0, q1, q2, k0, k1, k2, v, mqk, mv, mo, bg, gates):
    xb = _rms(x_ref[...], gain_ref[...]).astype(BF16)

    def proj(a, b):
        return jnp.dot(xb, w_ref[:, a:b], preferred_element_type=F32)

    for g, ref in enumerate((q0, q1, q2)):
        ref[...] = (proj(g * ATTN_W, (g + 1) * ATTN_W) * (HEAD_DIM ** -0.5)).astype(BF16)
    for g, ref in enumerate((k0, k1, k2)):
        ref[...] = proj(_OFFS[1] + g * ATTN_W, _OFFS[1] + (g + 1) * ATTN_W).astype(BF16)
    v[...] = proj(_OFFS[2], _OFFS[3]).astype(BF16)
    mqk[...] = proj(_OFFS[3], _OFFS[4]).astype(BF16)
    mv[...] = proj(_OFFS[4], _OFFS[5]).astype(BF16)
    mo[...] = proj(_OFFS[5], _OFFS[6]).astype(BF16)
    bg[...] = proj(_OFFS[6], _OFFS[6] + 2 * D_MODEL).astype(BF16)
    gp = jnp.dot(xb, wg_ref[...], preferred_element_type=F32) + bias_ref[...]
    col = lax.broadcasted_iota(jnp.int32, gp.shape, 1)
    is_forget = ((col // M_HEADS) % 2) == 1
    log_sig = jnp.minimum(gp, 0.0) - jnp.log1p(jnp.exp(-jnp.abs(gp)))
    gates[...] = jnp.where(is_forget, log_sig, gp)


def _inproj(x, gain, w_main, w_gate, bias, tm):
    t = x.shape[0]
    row = lambda w: pl.BlockSpec((tm, w), lambda i: (i, 0))
    widths = (ATTN_W,) * 7 + (2 * M_WIDTH, M_WIDTH, M_WIDTH, 2 * D_MODEL)
    out_shape = tuple(jax.ShapeDtypeStruct((t, w), BF16) for w in widths) + (
        jax.ShapeDtypeStruct((t, 4 * M_HEADS), F32),)
    return pl.pallas_call(
        _inproj_kernel,
        grid=(t // tm,),
        in_specs=[row(D_MODEL), _resident((1, D_MODEL)), _resident((D_MODEL, N_MAIN)),
                  _resident((D_MODEL, 4 * M_HEADS)), _resident((1, 4 * M_HEADS))],
        out_specs=tuple(row(w) for w in widths) + (row(4 * M_HEADS),),
        out_shape=out_shape,
        compiler_params=_params("parallel"),
        name="inproj",
    )(x, gain, w_main, w_gate, bias)


ATTN_TILE = 128


def _attn_kernel(q_ref, k_ref, kp_ref, kn_ref, v_ref, vp_ref, vn_ref, o_ref, lse_ref, kw, vw,
                 *, slopes, dilation, tqb, sub_len):
    i = pl.program_id(2)
    kw[0:HALF_WIN, :] = kp_ref[...]
    kw[HALF_WIN:HALF_WIN + tqb, :] = k_ref[...]
    kw[HALF_WIN + tqb:, :] = kn_ref[...]
    vw[0:HALF_WIN, :] = vp_ref[...]
    vw[HALF_WIN:HALF_WIN + tqb, :] = v_ref[...]
    vw[HALF_WIN + tqb:, :] = vn_ref[...]

    nk = ATTN_TILE + 2 * HALF_WIN
    qi = lax.broadcasted_iota(jnp.int32, (ATTN_TILE, nk), 0)
    kj = lax.broadcasted_iota(jnp.int32, (ATTN_TILE, nk), 1)
    step = kj - HALF_WIN - qi
    band = jnp.abs(step) <= HALF_WIN
    dist = (jnp.abs(step) * dilation).astype(F32)
    lane = lax.broadcasted_iota(jnp.int32, (ATTN_TILE, 2 * HEAD_DIM), 1)
    low = lane < HEAD_DIM

    for t in range(tqb // ATTN_TILE):
        kpos = i * tqb + (t * ATTN_TILE - HALF_WIN) + kj
        valid = band & (kpos >= 0) & (kpos < sub_len)
        mask_bias = jnp.where(valid, 0.0, NEG)
        rows = slice(t * ATTN_TILE, (t + 1) * ATTN_TILE)
        krows = slice(t * ATTN_TILE, t * ATTN_TILE + nk)
        for pair in range(HEADS_PER_GROUP // 2):
            cols = slice(pair * 2 * HEAD_DIM, (pair + 1) * 2 * HEAD_DIM)
            q2 = q_ref[rows, cols]
            k2 = kw[krows, cols]
            v2 = vw[krows, cols]
            outs = []
            for half in range(2):
                h = 2 * pair + half
                qh = jnp.where(low if half == 0 else ~low, q2, jnp.zeros_like(q2))
                s = lax.dot_general(qh, k2, (((1,), (1,)), ((), ())), preferred_element_type=F32)
                s = s - float(slopes[h]) * dist + mask_bias
                m = jnp.max(s, axis=-1, keepdims=True)
                p = jnp.exp(s - m)
                l = jnp.sum(p, axis=-1, keepdims=True)
                o = jnp.dot(p.astype(BF16), v2, preferred_element_type=F32) / l
                outs.append(o)
                lse_ref[rows, h:h + 1] = m + jnp.log(l)
            o_ref[rows, cols] = jnp.where(low, outs[0], outs[1]).astype(o_ref.dtype)


def _attention_group(q, k, v, g, nb, s):
    t = q.shape[0]
    dilation = ATTN_GROUPS[g][1]
    sub_len = s // dilation
    tqb = min(512, sub_len)
    nq = sub_len // tqb
    rows = t // dilation
    hb = tqb // HALF_WIN
    n_halo = rows // HALF_WIN
    view = lambda a: a.reshape(rows, dilation * ATTN_W)

    main = pl.BlockSpec((tqb, ATTN_W), lambda b, c, i: (b * nq + i, c))
    prev = pl.BlockSpec((HALF_WIN, ATTN_W), lambda b, c, i: (jnp.maximum((b * nq + i) * hb - 1, 0), c))
    nxt = pl.BlockSpec((HALF_WIN, ATTN_W), lambda b, c, i: (jnp.minimum((b * nq + i + 1) * hb, n_halo - 1), c))
    kern = functools.partial(_attn_kernel, slopes=tuple(_alibi_slopes()[g]), dilation=dilation,
                             tqb=tqb, sub_len=sub_len)
    out, lse = pl.pallas_call(
        kern,
        grid=(nb, dilation, nq),
        in_specs=[main, main, prev, nxt, main, prev, nxt],
        out_specs=(main, pl.BlockSpec((None, tqb, HEADS_PER_GROUP), lambda b, c, i: (c, b * nq + i, 0))),
        out_shape=(jax.ShapeDtypeStruct((rows, dilation * ATTN_W), BF16),
                   jax.ShapeDtypeStruct((dilation, rows, HEADS_PER_GROUP), F32)),
        scratch_shapes=[pltpu.VMEM((tqb + 2 * HALF_WIN, ATTN_W), BF16),
                        pltpu.VMEM((tqb + 2 * HALF_WIN, ATTN_W), BF16)],
        compiler_params=_params("parallel", "parallel", "parallel"),
        name=f"attn_g{g}",
    )(view(q), view(k), view(k), view(k), view(v), view(v), view(v))
    return out.reshape(t, ATTN_W), lse.transpose(1, 0, 2).reshape(t, HEADS_PER_GROUP)


CONV_HALO = 8


def _conv_kernel(x_ref, xp_ref, xn_ref, w_ref, b_ref, o_ref, ext, *, rows):
    j = pl.program_id(1)
    cb = pl.program_id(2)
    zero = jnp.zeros((CONV_HALO, x_ref.shape[1]), F32)
    ext[0:CONV_HALO, :] = jnp.where(j == 0, zero, xp_ref[...].astype(F32))
    ext[CONV_HALO:CONV_HALO + rows, :] = x_ref[...].astype(F32)
    ext[CONV_HALO + rows:, :] = jnp.where(j == pl.num_programs(1) - 1, zero, xn_ref[...].astype(F32))
    acc = jnp.zeros((rows, x_ref.shape[1]), F32) + b_ref[...]
    for tap in range(CONV_WIDTH):
        off = CONV_HALO - CONV_WIDTH // 2 + tap
        acc = acc + ext[off:off + rows, :] * w_ref[tap:tap + 1, :]
    y = acc * jax.nn.sigmoid(acc)
    n_q_blocks = M_WIDTH // x_ref.shape[1]
    scale = jnp.where(cb < n_q_blocks, M_HEAD_DIM ** -0.5, 1.0).astype(F32)
    o_ref[...] = (y * scale).astype(o_ref.dtype)


def _conv_silu(x, w, bias, nb, s):
    t, width = x.shape
    rows = 512
    cw = 512
    nr = s // rows
    hb = rows // CONV_HALO
    n_halo = t // CONV_HALO
    main = pl.BlockSpec((rows, cw), lambda b, j, c: (b * nr + j, c))
    prev = pl.BlockSpec((CONV_HALO, cw), lambda b, j, c: (jnp.maximum((b * nr + j) * hb - 1, 0), c))
    nxt = pl.BlockSpec((CONV_HALO, cw), lambda b, j, c: (jnp.minimum((b * nr + j + 1) * hb, n_halo - 1), c))
    return pl.pallas_call(
        functools.partial(_conv_kernel, rows=rows),
        grid=(nb, nr, width // cw),
        in_specs=[main, prev, nxt,
                  pl.BlockSpec((CONV_WIDTH, cw), lambda b, j, c: (0, c)),
                  pl.BlockSpec((1, cw), lambda b, j, c: (0, c))],
        out_specs=main,
        out_shape=jax.ShapeDtypeStruct((t, width), BF16),
        scratch_shapes=[pltpu.VMEM((rows + 2 * CONV_HALO, cw), F32)],
        compiler_params=_params("parallel", "parallel", "parallel"),
        name="conv_silu",
    )(x, x, x, w, bias)


def _mlstm_kernel(q_ref, k_ref, v_ref, g_ref, h_ref, c_scr, n_scr, m_scr, *, reverse, n_chunks):
    @pl.when(pl.program_id(2) == 0)
    def _():
        c_scr[...] = jnp.zeros_like(c_scr)
        n_scr[...] = jnp.zeros_like(n_scr)
        m_scr[...] = jnp.zeros_like(m_scr)

    L = M_CHUNK
    ti = lax.broadcasted_iota(jnp.int32, (L, L), 0)
    si = lax.broadcasted_iota(jnp.int32, (L, L), 1)
    eye = ti == si
    tri = (si >= ti) if reverse else (si <= ti)
    gi, gf = (2, 3) if reverse else (0, 1)

    def to_col(row):
        return jnp.sum(jnp.where(eye, row, 0.0), axis=1, keepdims=True)

    def to_row(col):
        return jnp.sum(jnp.where(eye, col, 0.0), axis=0, keepdims=True)

    c_state = c_scr[...]
    n_state = n_scr[...]
    m_prev = m_scr[...]
    order = range(n_chunks - 1, -1, -1) if reverse else range(n_chunks)
    for c in order:
        rows = slice(c * L, (c + 1) * L)
        q = q_ref[rows, :]
        k = k_ref[rows, :]
        v = v_ref[rows, :]
        li_row = g_ref[gi:gi + 1, rows]
        lf_row = g_ref[gf:gf + 1, rows]
        li_col = to_col(li_row)
        cum_col = jnp.sum(jnp.where(tri, lf_row, 0.0), axis=1, keepdims=True)
        cum_row = to_row(cum_col)
        tot = jnp.sum(lf_row, axis=1, keepdims=True)
        dmat = jnp.where(tri, cum_col - cum_row + li_row, NEG)
        m_inter = cum_col + m_prev
        m_t = jnp.maximum(m_inter, jnp.max(dmat, axis=1, keepdims=True))
        w_inter = jnp.exp(m_inter - m_t)
        qk = lax.dot_general(q, k, (((1,), (1,)), ((), ())), preferred_element_type=F32)
        sc = qk * jnp.exp(dmat - m_t)
        num = (jnp.dot(sc.astype(BF16), v, preferred_element_type=F32)
               + w_inter * jnp.dot(q, c_state.astype(BF16), preferred_element_type=F32))
        den = (jnp.sum(sc, axis=1, keepdims=True)
               + w_inter * jnp.sum(q.astype(F32) * n_state, axis=1, keepdims=True))
        h_ref[rows, :] = (num / jnp.maximum(jnp.abs(den), jnp.exp(-m_t))).astype(h_ref.dtype)
        a = tot - cum_col + li_col
        m_new = jnp.maximum(tot + m_prev, jnp.max(a, axis=0, keepdims=True))
        kw = k.astype(F32) * jnp.exp(a - m_new)
        decay = jnp.exp(tot + m_prev - m_new)
        c_state = decay * c_state + lax.dot_general(kw.astype(BF16), v, (((0,), (0,)), ((), ())),
                                                    preferred_element_type=F32)
        n_state = decay * n_state + jnp.sum(kw, axis=0, keepdims=True)
        m_prev = m_new
    c_scr[...] = c_state
    n_scr[...] = n_state
    m_scr[...] = m_prev


def _mlstm_direction(qk, v, gates_t, nb, s, reverse):
    t = v.shape[0]
    n_chunks = 4
    rows = n_chunks * M_CHUNK
    nr = s // rows
    blk = (lambda b, j: b * nr + (nr - 1 - j)) if reverse else (lambda b, j: b * nr + j)
    return pl.pallas_call(
        functools.partial(_mlstm_kernel, reverse=reverse, n_chunks=n_chunks),
        grid=(nb, M_HEADS, nr),
        in_specs=[pl.BlockSpec((rows, M_HEAD_DIM), lambda b, h, j: (blk(b, j), h)),
                  pl.BlockSpec((rows, M_HEAD_DIM), lambda b, h, j: (blk(b, j), M_HEADS + h)),
                  pl.BlockSpec((rows, M_HEAD_DIM), lambda b, h, j: (blk(b, j), h)),
                  pl.BlockSpec((None, 4, rows), lambda b, h, j: (h, 0, blk(b, j)))],
        out_specs=pl.BlockSpec((rows, M_HEAD_DIM), lambda b, h, j: (blk(b, j), h)),
        out_shape=jax.ShapeDtypeStruct((t, M_WIDTH), BF16),
        scratch_shapes=[pltpu.VMEM((M_HEAD_DIM, M_HEAD_DIM), F32), pltpu.VMEM((1, M_HEAD_DIM), F32),
                        pltpu.VMEM((1, 1), F32)],
        compiler_params=_params("parallel", "parallel", "arbitrary"),
        name="mlstm_bwd" if reverse else "mlstm_fwd",
    )(qk, qk, v, gates_t)


LSE_PAD = 128


def _mix_out_kernel(o0, o1, o2, l0, l1, l2, hf, hb, mo, bg, x_ref, wa, wm, wo, gn, gpost, y_ref, wpad):
    lses = [l0[...], l1[...], l2[...]]
    mx = jnp.maximum(jnp.maximum(lses[0], lses[1]), lses[2])
    es = [jnp.exp(l - mx) for l in lses]
    z = es[0] + es[1] + es[2]
    hd = lax.broadcasted_iota(jnp.int32, (LSE_PAD, ATTN_W), 0)
    ln = lax.broadcasted_iota(jnp.int32, (LSE_PAD, ATTN_W), 1)
    expand = jnp.where(hd == ln // HEAD_DIM, 1.0, 0.0).astype(BF16)
    attn = jnp.zeros(o0.shape, F32)
    wpad[...] = jnp.zeros_like(wpad)
    for e, o in zip(es, (o0, o1, o2)):
        w = e / z
        w_hi = w.astype(BF16)
        w_lo = (w - w_hi.astype(F32)).astype(BF16)
        wpad[:, 0:HEADS_PER_GROUP] = w_hi.astype(F32)
        wide = jnp.dot(wpad[...].astype(BF16), expand, preferred_element_type=F32)
        wpad[:, 0:HEADS_PER_GROUP] = w_lo.astype(F32)
        wide = wide + jnp.dot(wpad[...].astype(BF16), expand, preferred_element_type=F32)
        attn = attn + wide * o[...].astype(F32)
    hs = hf[...].astype(F32) + hb[...].astype(F32)
    parts = []
    for h in range(M_HEADS):
        p = hs[:, h * M_HEAD_DIM:(h + 1) * M_HEAD_DIM]
        parts.append(p * lax.rsqrt(jnp.mean(p * p, axis=-1, keepdims=True) + EPS))
    hn = jnp.concatenate(parts, axis=-1) * gn[...]
    ml = jax.nn.sigmoid(mo[...].astype(F32)) * hn
    gates = jax.nn.sigmoid(bg[...].astype(F32))
    merged = (gates[:, :D_MODEL] * jnp.dot(attn.astype(BF16), wa[...], preferred_element_type=F32)
              + gates[:, D_MODEL:] * jnp.dot(ml.astype(BF16), wm[...], preferred_element_type=F32))
    mix = jnp.dot(merged.astype(BF16), wo[...], preferred_element_type=F32)
    y_ref[...] = x_ref[...] + _rms(mix, gpost[...])


def _mix_out(outs, lses, hf, hb, mo, bg, x, wa, wm, wo, gn, gpost, tm):
    t = x.shape[0]
    row = lambda w: pl.BlockSpec((tm, w), lambda i: (i, 0))
    return pl.pallas_call(
        _mix_out_kernel,
        grid=(t // tm,),
        in_specs=[row(ATTN_W)] * 3 + [row(HEADS_PER_GROUP)] * 3
                 + [row(M_WIDTH), row(M_WIDTH), row(M_WIDTH), row(2 * D_MODEL), row(D_MODEL),
                    _resident((ATTN_W, D_MODEL)), _resident((M_WIDTH, D_MODEL)), _resident((D_MODEL, D_MODEL)),
                    _resident((1, M_WIDTH)), _resident((1, D_MODEL))],
        out_specs=row(D_MODEL),
        out_shape=jax.ShapeDtypeStruct((t, D_MODEL), F32),
        scratch_shapes=[pltpu.VMEM((tm, LSE_PAD), F32)],
        compiler_params=_params("parallel"),
        name="mix_out",
    )(*outs, *lses, hf, hb, mo, bg, x, wa, wm, wo, gn, gpost)


FFN_CHUNK = 1408


def _ffn_kernel(x_ref, gpre, win, wout, gpost, y_ref):
    x = x_ref[...]
    xb = _rms(x, gpre[...]).astype(BF16)
    acc = jnp.zeros(x.shape, F32)
    for c in range(FFN_HIDDEN // FFN_CHUNK):
        lo = c * FFN_CHUNK
        gate = jnp.dot(xb, win[:, lo:lo + FFN_CHUNK], preferred_element_type=F32)
        up = jnp.dot(xb, win[:, FFN_HIDDEN + lo:FFN_HIDDEN + lo + FFN_CHUNK], preferred_element_type=F32)
        act = (gate * jax.nn.sigmoid(gate) * up).astype(BF16)
        acc = acc + jnp.dot(act, wout[lo:lo + FFN_CHUNK, :], preferred_element_type=F32)
    y_ref[...] = x + _rms(acc, gpost[...])


def _ffn(x, gpre, win, wout, gpost, tm):
    t = x.shape[0]
    row = pl.BlockSpec((tm, D_MODEL), lambda i: (i, 0))
    return pl.pallas_call(
        _ffn_kernel,
        grid=(t // tm,),
        in_specs=[row, _resident((1, D_MODEL)), _resident((D_MODEL, 2 * FFN_HIDDEN)),
                  _resident((FFN_HIDDEN, D_MODEL)), _resident((1, D_MODEL))],
        out_specs=row,
        out_shape=jax.ShapeDtypeStruct((t, D_MODEL), F32),
        compiler_params=_params("parallel"),
        name="ffn",
    )(x, gpre, win, wout, gpost)


def _layer(x, nb, s, p):
    q0, q1, q2, k0, k1, k2, v, mqk, mv, mo, bg, gates = _inproj(
        x, p["norm_mix_pre"], p["w_main"], p["w_gate"], p["b_gate"], tm=256)
    attn = [_attention_group(q, k, v, g, nb, s) for g, (q, k) in enumerate(((q0, k0), (q1, k1), (q2, k2)))]
    qk = _conv_silu(mqk, p["w_conv"], p["b_conv"], nb, s)
    t = x.shape[0]
    gates_t = gates.reshape(t, 4, M_HEADS).transpose(2, 1, 0)
    hf = _mlstm_direction(qk, mv, gates_t, nb, s, reverse=False)
    hb = _mlstm_direction(qk, mv, gates_t, nb, s, reverse=True)
    x = _mix_out([a[0] for a in attn], [a[1] for a in attn], hf, hb, mo, bg, x,
                 p["w_attn_proj"], p["w_mlstm_proj"], p["w_out"], p["g_mlstm_norm"], p["norm_mix_post"], tm=256)
    return _ffn(x, p["norm_ffn_pre"], p["w_ffn_in"], p["w_ffn_out"], p["norm_ffn_post"], tm=512)


def _prepare(norm_mix_pre, norm_mix_post, norm_ffn_pre, norm_ffn_post, w_in, b_mlstm_gates, w_conv, b_conv,
             g_mlstm_norm, w_attn_proj, w_mlstm_proj, w_out, w_ffn_in, w_ffn_out):
    g0, g1 = _OFFS[6], _OFFS[7]
    w_main = jnp.concatenate([w_in[:, :, :g0], w_in[:, :, g1:]], axis=-1).astype(BF16)
    w_gate = w_in[:, :, g0:g1].astype(BF16)
    layers = []
    for l in range(w_in.shape[0]):
        layers.append(dict(
            norm_mix_pre=norm_mix_pre[l][None], norm_mix_post=norm_mix_post[l][None],
            norm_ffn_pre=norm_ffn_pre[l][None], norm_ffn_post=norm_ffn_post[l][None],
            w_main=w_main[l], w_gate=w_gate[l], b_gate=b_mlstm_gates[l][None],
            w_conv=w_conv[l], b_conv=b_conv[l][None], g_mlstm_norm=g_mlstm_norm[l][None],
            w_attn_proj=w_attn_proj[l].astype(BF16), w_mlstm_proj=w_mlstm_proj[l].astype(BF16),
            w_out=w_out[l].astype(BF16), w_ffn_in=w_ffn_in[l].astype(BF16), w_ffn_out=w_ffn_out[l].astype(BF16)))
    return layers


def _trunk(x, layers):
    nb, s, d = x.shape
    y = x.reshape(nb * s, d)
    for p in layers:
        y = _layer(y, nb, s, p)
    return y.reshape(nb, s, d)


def kernel(x_prompt, x_sample, norm_mix_pre, norm_mix_post, norm_ffn_pre, norm_ffn_post, w_in, b_mlstm_gates,
           w_conv, b_conv, g_mlstm_norm, w_attn_proj, w_mlstm_proj, w_out, w_ffn_in, w_ffn_out):
    layers = _prepare(norm_mix_pre, norm_mix_post, norm_ffn_pre, norm_ffn_post, w_in, b_mlstm_gates, w_conv,
                      b_conv, g_mlstm_norm, w_attn_proj, w_mlstm_proj, w_out, w_ffn_in, w_ffn_out)
    return _trunk(x_prompt, layers), _trunk(x_sample, layers)
```

```python
import functools

import numpy as np
import jax
import jax.numpy as jnp
from jax import lax
from jax.experimental import pallas as pl
from jax.experimental.pallas import tpu as pltpu

F32 = jnp.float32
BF16 = jnp.bfloat16

D_MODEL = 1024
ATTN_GROUPS = ((128, 1), (512, 4), (2048, 16))
DILATIONS = tuple(d for _, d in ATTN_GROUPS)
N_GROUPS = 3
HEADS_PER_GROUP = 8
HEAD_DIM = 64
HALF_WIN = 64
ATTN_W = HEADS_PER_GROUP * HEAD_DIM
M_WIDTH = 1024
M_HEADS = 4
M_HEAD_DIM = 256
M_CHUNK = 128
CONV_WIDTH = 5
FFN_HIDDEN = 2816
EPS = 1e-6
NEG = -1e30
SUBLANES = 8
GATE_ROWS = M_HEADS * SUBLANES

OFF_QK = 0
OFF_V = OFF_QK + 2 * N_GROUPS * ATTN_W
OFF_MQK = OFF_V + ATTN_W
OFF_MV = OFF_MQK + 2 * M_WIDTH
OFF_MO = OFF_MV + M_WIDTH
OFF_BG = OFF_MO + M_WIDTH
N_MAIN = OFF_BG + 2 * D_MODEL

VMEM_LIMIT = 56 * 1024 * 1024


def _alibi_slopes():
    h = np.arange(1, N_GROUPS * HEADS_PER_GROUP + 1, dtype=np.float32)
    return (2.0 ** (-8.0 * h / (N_GROUPS * HEADS_PER_GROUP))).astype(np.float32).reshape(N_GROUPS, HEADS_PER_GROUP)


def _rms(x, gain):
    return x * lax.rsqrt(jnp.mean(x * x, axis=-1, keepdims=True) + EPS) * gain


def _resident(shape):
    nd = len(shape)
    return pl.BlockSpec(shape, lambda *_: (0,) * nd, pipeline_mode=pl.Buffered(1))


LANES = 128


def _slab_store(ref, row0, value):
    rows = value.shape[0]
    for j in range(ref.shape[0]):
        ref[j, row0:row0 + rows, :] = value[:, j * LANES:(j + 1) * LANES]


def _slab_rows(ref, start, size, stride):
    return jnp.concatenate([ref[j, pl.ds(start, size, stride=stride), :] for j in range(ref.shape[0])], axis=1)


def _params(*sem):
    return pltpu.CompilerParams(dimension_semantics=sem, vmem_limit_bytes=VMEM_LIMIT)


def _inproj_kernel(x_ref, xp_ref, xn_ref, gain_ref, w_ref, wg_ref, bias_ref, wc_ref, bc_ref,
                   qk0, qk1, qk2, v0, v1, v2, mqk, mv, mo, bg, gates, xs, vs, *, tm, tiles_per_seq):
    i = pl.program_id(0)
    gain = gain_ref[...]
    first = (i % tiles_per_seq) == 0
    last = (i % tiles_per_seq) == tiles_per_seq - 1
    xn = _rms(x_ref[...], gain)
    x_prev = jnp.where(first, 0.0, _rms(xp_ref[...], gain))
    x_next = jnp.where(last, 0.0, _rms(xn_ref[...], gain))
    _slab_store(xs, 0, xn)
    xb = xn.astype(BF16)

    def proj(lhs, a, b):
        return jnp.dot(lhs, w_ref[:, a:b], preferred_element_type=F32)

    qk0[...] = proj(xb, OFF_QK, OFF_QK + 2 * ATTN_W).astype(BF16)
    for g, ref in ((1, qk1), (2, qk2)):
        d = DILATIONS[g]
        n = tm // d
        xperm = jnp.concatenate([_slab_rows(xs, c, n, d) for c in range(d)], axis=0)
        pg = proj(xperm.astype(BF16), OFF_QK + g * 2 * ATTN_W, OFF_QK + (g + 1) * 2 * ATTN_W).astype(BF16)
        for c in range(d):
            ref[:, c * 2 * ATTN_W:(c + 1) * 2 * ATTN_W] = pg[c * n:(c + 1) * n, :]
    vf = proj(xb, OFF_V, OFF_V + ATTN_W)
    v0[...] = vf.astype(BF16)
    _slab_store(vs, 0, vf)
    for g, ref in ((1, v1), (2, v2)):
        d = DILATIONS[g]
        for c in range(d):
            ref[:, c * ATTN_W:(c + 1) * ATTN_W] = _slab_rows(vs, c, tm // d, d).astype(BF16)
    xe = jnp.concatenate([x_prev, xn, x_next], axis=0).astype(BF16)
    ext = tm + 2 * SUBLANES
    cw = ATTN_W
    for cb in range(2 * M_WIDTH // cw):
        pm = proj(xe, OFF_MQK + cb * cw, OFF_MQK + (cb + 1) * cw)
        acc = bc_ref[:, cb * cw:(cb + 1) * cw]
        for tap in range(CONV_WIDTH):
            shift = (CONV_WIDTH // 2 - tap) % ext
            shifted = pm if shift == 0 else pltpu.roll(pm, shift, 0)
            acc = acc + shifted[SUBLANES:SUBLANES + tm, :] * wc_ref[tap:tap + 1, cb * cw:(cb + 1) * cw]
        y = acc * jax.nn.sigmoid(acc)
        if cb < M_WIDTH // cw:
            y = y * (M_HEAD_DIM ** -0.5)
        mqk[:, cb * cw:(cb + 1) * cw] = y.astype(BF16)
    mv[...] = proj(xb, OFF_MV, OFF_MV + M_WIDTH).astype(BF16)
    mo[...] = proj(xb, OFF_MO, OFF_MO + M_WIDTH).astype(BF16)
    bg[...] = proj(xb, OFF_BG, OFF_BG + 2 * D_MODEL).astype(BF16)
    gp = lax.dot_general(wg_ref[...], xb, (((1,), (1,)), ((), ())), preferred_element_type=F32) + bias_ref[...]
    row = lax.broadcasted_iota(jnp.int32, gp.shape, 0)
    log_sig = jnp.minimum(gp, 0.0) - jnp.log1p(jnp.exp(-jnp.abs(gp)))
    gates[...] = jnp.where(row % 2 == 1, log_sig, gp)


def _inproj(x, p, s, tm):
    t = x.shape[0]
    hb = tm // SUBLANES
    n_halo = t // SUBLANES
    row = lambda w: pl.BlockSpec((tm, w), lambda i: (i, 0))
    view = lambda d, w: pl.BlockSpec((tm // d, d * w), lambda i: (i, 0))
    out_specs = (row(2 * ATTN_W), view(4, 2 * ATTN_W), view(16, 2 * ATTN_W),
                 row(ATTN_W), view(4, ATTN_W), view(16, ATTN_W),
                 row(2 * M_WIDTH), row(M_WIDTH), row(M_WIDTH), row(2 * D_MODEL),
                 pl.BlockSpec((GATE_ROWS, tm), lambda i: (0, i)))
    bshape = lambda d, w: jax.ShapeDtypeStruct((t // d, d * w), BF16)
    out_shape = (bshape(1, 2 * ATTN_W), bshape(4, 2 * ATTN_W), bshape(16, 2 * ATTN_W),
                 bshape(1, ATTN_W), bshape(4, ATTN_W), bshape(16, ATTN_W),
                 bshape(1, 2 * M_WIDTH), bshape(1, M_WIDTH), bshape(1, M_WIDTH), bshape(1, 2 * D_MODEL),
                 jax.ShapeDtypeStruct((GATE_ROWS, t), F32))
    return pl.pallas_call(
        functools.partial(_inproj_kernel, tm=tm, tiles_per_seq=s // tm),
        grid=(t // tm,),
        in_specs=[row(D_MODEL),
                  pl.BlockSpec((SUBLANES, D_MODEL), lambda i: (jnp.maximum(i * hb - 1, 0), 0)),
                  pl.BlockSpec((SUBLANES, D_MODEL), lambda i: (jnp.minimum((i + 1) * hb, n_halo - 1), 0)),
                  _resident((1, D_MODEL)), _resident((D_MODEL, N_MAIN)), _resident((GATE_ROWS, D_MODEL)),
                  _resident((GATE_ROWS, 1)), _resident((CONV_WIDTH, 2 * M_WIDTH)), _resident((1, 2 * M_WIDTH))],
        out_specs=out_specs,
        out_shape=out_shape,
        scratch_shapes=[pltpu.VMEM((D_MODEL // LANES, tm, LANES), F32), pltpu.VMEM((ATTN_W // LANES, tm, LANES), F32)],
        compiler_params=_params("parallel"),
        name="inproj",
    )(x, x, x, p["norm_mix_pre"], p["w_main"], p["w_gate"], p["b_gate"], p["w_conv"], p["b_conv"])


ATTN_TILE = 128


def _attn_kernel(q_ref, k_ref, kp_ref, kn_ref, v_ref, vp_ref, vn_ref, o_ref, lse_ref, kw, vw,
                 *, slopes, dilation, tqb, sub_len):
    i = pl.program_id(2)
    kw[0:HALF_WIN, :] = kp_ref[...]
    kw[HALF_WIN:HALF_WIN + tqb, :] = k_ref[...]
    kw[HALF_WIN + tqb:, :] = kn_ref[...]
    vw[0:HALF_WIN, :] = vp_ref[...]
    vw[HALF_WIN:HALF_WIN + tqb, :] = v_ref[...]
    vw[HALF_WIN + tqb:, :] = vn_ref[...]

    nk = ATTN_TILE + 2 * HALF_WIN
    qi = lax.broadcasted_iota(jnp.int32, (ATTN_TILE, nk), 0)
    kj = lax.broadcasted_iota(jnp.int32, (ATTN_TILE, nk), 1)
    step = kj - HALF_WIN - qi
    band = jnp.abs(step) <= HALF_WIN
    dist = (jnp.abs(step) * dilation).astype(F32)
    lane = lax.broadcasted_iota(jnp.int32, (ATTN_TILE, 2 * HEAD_DIM), 1)
    low = lane < HEAD_DIM

    for t in range(tqb // ATTN_TILE):
        kpos = i * tqb + (t * ATTN_TILE - HALF_WIN) + kj
        valid = band & (kpos >= 0) & (kpos < sub_len)
        mask_bias = jnp.where(valid, 0.0, NEG)
        rows = slice(t * ATTN_TILE, (t + 1) * ATTN_TILE)
        krows = slice(t * ATTN_TILE, t * ATTN_TILE + nk)
        for pair in range(HEADS_PER_GROUP // 2):
            cols = slice(pair * 2 * HEAD_DIM, (pair + 1) * 2 * HEAD_DIM)
            q2 = q_ref[rows, cols]
            k2 = kw[krows, cols]
            v2 = vw[krows, cols]
            outs = []
            for half in range(2):
                h = 2 * pair + half
                qh = jnp.where(low if half == 0 else ~low, q2, jnp.zeros_like(q2))
                s = lax.dot_general(qh, k2, (((1,), (1,)), ((), ())), preferred_element_type=F32)
                s = s - float(slopes[h]) * dist + mask_bias
                m = jnp.max(s, axis=-1, keepdims=True)
                p = jnp.exp(s - m)
                l = jnp.sum(p, axis=-1, keepdims=True)
                o = jnp.dot(p.astype(BF16), v2, preferred_element_type=F32) / l
                outs.append(o)
                lse_ref[rows, h:h + 1] = m + jnp.log(l)
            o_ref[rows, cols] = jnp.where(low, outs[0], outs[1]).astype(o_ref.dtype)


def _attention_group(qk, v, g, nb, s):
    dilation = DILATIONS[g]
    rows = qk.shape[0]
    sub_len = s // dilation
    tqb = min(512, sub_len)
    nq = sub_len // tqb
    hb = tqb // HALF_WIN
    n_halo = rows // HALF_WIN

    def spec(size, row_fn, col_fn):
        return pl.BlockSpec((size, ATTN_W), lambda b, c, i: (row_fn(b * nq + i), col_fn(c)))

    cur = lambda r: r
    prv = lambda r: jnp.maximum(r * hb - 1, 0)
    nxt = lambda r: jnp.minimum((r + 1) * hb, n_halo - 1)
    qcol, kcol, vcol = (lambda c: 2 * c), (lambda c: 2 * c + 1), (lambda c: c)
    kern = functools.partial(_attn_kernel, slopes=tuple(_alibi_slopes()[g]), dilation=dilation,
                             tqb=tqb, sub_len=sub_len)
    return pl.pallas_call(
        kern,
        grid=(nb, dilation, nq),
        in_specs=[spec(tqb, cur, qcol), spec(tqb, cur, kcol), spec(HALF_WIN, prv, kcol), spec(HALF_WIN, nxt, kcol),
                  spec(tqb, cur, vcol), spec(HALF_WIN, prv, vcol), spec(HALF_WIN, nxt, vcol)],
        out_specs=(spec(tqb, cur, vcol),
                   pl.BlockSpec((None, tqb, HEADS_PER_GROUP), lambda b, c, i: (c, b * nq + i, 0))),
        out_shape=(jax.ShapeDtypeStruct((rows, dilation * ATTN_W), BF16),
                   jax.ShapeDtypeStruct((dilation, rows, HEADS_PER_GROUP), F32)),
        scratch_shapes=[pltpu.VMEM((tqb + 2 * HALF_WIN, ATTN_W), BF16),
                        pltpu.VMEM((tqb + 2 * HALF_WIN, ATTN_W), BF16)],
        compiler_params=_params("parallel", "parallel", "parallel"),
        name=f"attn_g{g}",
    )(qk, qk, qk, qk, v, v, v)


MLSTM_CHUNKS = 4


def _mlstm_chunk(q, k, v, li_row, lf_row, state, tri, eye):
    c_prev, n_prev, m_prev = state
    cum_col = jnp.sum(jnp.where(tri, lf_row, 0.0), axis=1, keepdims=True)
    cum_row = jnp.sum(jnp.where(eye, cum_col, 0.0), axis=0, keepdims=True)
    li_col = jnp.sum(jnp.where(eye, li_row, 0.0), axis=1, keepdims=True)
    tot = jnp.sum(lf_row, axis=1, keepdims=True)
    dmat = jnp.where(tri, cum_col - cum_row + li_row, NEG)
    m_loc = jnp.max(dmat, axis=1, keepdims=True)
    qk = lax.dot_general(q, k, (((1,), (1,)), ((), ())), preferred_element_type=F32)
    sc = qk * jnp.exp(dmat - m_loc)
    num_loc = jnp.dot(sc.astype(BF16), v, preferred_element_type=F32)
    den_loc = jnp.sum(sc, axis=1, keepdims=True)
    a = tot - cum_col + li_col
    a_max = jnp.max(a, axis=0, keepdims=True)
    kw = k.astype(F32) * jnp.exp(a - a_max)
    upd = lax.dot_general(kw.astype(BF16), v, (((0,), (0,)), ((), ())), preferred_element_type=F32)
    n_upd = jnp.sum(kw, axis=0, keepdims=True)
    m_inter = cum_col + m_prev
    m_t = jnp.maximum(m_inter, m_loc)
    r_loc = jnp.exp(m_loc - m_t)
    w_inter = jnp.exp(m_inter - m_t)
    num = r_loc * num_loc + w_inter * jnp.dot(q, c_prev.astype(BF16), preferred_element_type=F32)
    den = r_loc * den_loc + w_inter * jnp.sum(q.astype(F32) * n_prev, axis=1, keepdims=True)
    h = num * (1.0 / jnp.maximum(jnp.abs(den), jnp.exp(-m_t)))
    m_new = jnp.maximum(tot + m_prev, a_max)
    alpha = jnp.exp(tot + m_prev - m_new)
    beta = jnp.exp(a_max - m_new)
    return h, (alpha * c_prev + beta * upd, alpha * n_prev + beta * n_upd, m_new)


def _mlstm_kernel(qf, kf, vf, gf, qb, kb, vb, gb, hf_ref, hb_ref, c_scr, n_scr, m_scr):
    @pl.when(pl.program_id(2) == 0)
    def _():
        c_scr[...] = jnp.zeros_like(c_scr)
        n_scr[...] = jnp.zeros_like(n_scr)
        m_scr[...] = jnp.zeros_like(m_scr)

    L = M_CHUNK
    ti = lax.broadcasted_iota(jnp.int32, (L, L), 0)
    si = lax.broadcasted_iota(jnp.int32, (L, L), 1)
    eye = ti == si
    lower = si <= ti
    upper = si >= ti
    fwd = (c_scr[0], n_scr[0], m_scr[0])
    bwd = (c_scr[1], n_scr[1], m_scr[1])
    for c in range(MLSTM_CHUNKS):
        rows = slice(c * L, (c + 1) * L)
        h, fwd = _mlstm_chunk(qf[rows, :], kf[rows, :], vf[rows, :], gf[0:1, rows], gf[1:2, rows], fwd, lower, eye)
        hf_ref[rows, :] = h.astype(hf_ref.dtype)
        rows = slice((MLSTM_CHUNKS - 1 - c) * L, (MLSTM_CHUNKS - c) * L)
        h, bwd = _mlstm_chunk(qb[rows, :], kb[rows, :], vb[rows, :], gb[2:3, rows], gb[3:4, rows], bwd, upper, eye)
        hb_ref[rows, :] = h.astype(hb_ref.dtype)
    for d, st in enumerate((fwd, bwd)):
        c_scr[d] = st[0]
        n_scr[d] = st[1]
        m_scr[d] = st[2]


def _mlstm(qk, v, gates, nb, s):
    t = v.shape[0]
    rows = MLSTM_CHUNKS * M_CHUNK
    nr = s // rows
    fblk = lambda b, j: b * nr + j
    bblk = lambda b, j: b * nr + (nr - 1 - j)

    def specs(blk):
        return [pl.BlockSpec((rows, M_HEAD_DIM), lambda b, h, j: (blk(b, j), h)),
                pl.BlockSpec((rows, M_HEAD_DIM), lambda b, h, j: (blk(b, j), M_HEADS + h)),
                pl.BlockSpec((rows, M_HEAD_DIM), lambda b, h, j: (blk(b, j), h)),
                pl.BlockSpec((SUBLANES, rows), lambda b, h, j: (h, blk(b, j)))]

    out = lambda blk: pl.BlockSpec((rows, M_HEAD_DIM), lambda b, h, j: (blk(b, j), h))
    return pl.pallas_call(
        _mlstm_kernel,
        grid=(nb, M_HEADS, nr),
        in_specs=specs(fblk) + specs(bblk),
        out_specs=(out(fblk), out(bblk)),
        out_shape=(jax.ShapeDtypeStruct((t, M_WIDTH), BF16), jax.ShapeDtypeStruct((t, M_WIDTH), BF16)),
        scratch_shapes=[pltpu.VMEM((2, M_HEAD_DIM, M_HEAD_DIM), F32), pltpu.VMEM((2, 1, M_HEAD_DIM), F32),
                        pltpu.VMEM((2, 1, 1), F32)],
        compiler_params=_params("parallel", "parallel", "arbitrary"),
        name="mlstm",
    )(qk, qk, v, gates, qk, qk, v, gates)


LSE_PAD = 128


def _mix_out_kernel(o0, o1, o2, l0, l1, l2, hf, hb, mo, bg, x_ref, wa, wm, wo, gn, gpost, y_ref,
                    wpad, os1, os2, *, tm):
    for o, osc, d in ((o1, os1, DILATIONS[1]), (o2, os2, DILATIONS[2])):
        n = tm // d
        for c in range(d):
            for j in range(ATTN_W // LANES):
                lo = c * ATTN_W + j * LANES
                osc[j, pl.ds(c, n, stride=d), :] = o[:, lo:lo + LANES].astype(F32)
    lses = [l0[...], l1[...], l2[...]]
    outs = [o0[...].astype(F32)] + [jnp.concatenate([osc[j] for j in range(ATTN_W // LANES)], axis=1)
                                    for osc in (os1, os2)]
    mx = jnp.maximum(jnp.maximum(lses[0], lses[1]), lses[2])
    es = [jnp.exp(l - mx) for l in lses]
    z = es[0] + es[1] + es[2]
    hd = lax.broadcasted_iota(jnp.int32, (LSE_PAD, ATTN_W), 0)
    ln = lax.broadcasted_iota(jnp.int32, (LSE_PAD, ATTN_W), 1)
    expand = jnp.where(hd == ln // HEAD_DIM, 1.0, 0.0).astype(BF16)
    attn = jnp.zeros((tm, ATTN_W), F32)
    wpad[...] = jnp.zeros_like(wpad)
    for e, o in zip(es, outs):
        w = e / z
        w_hi = w.astype(BF16)
        w_lo = (w - w_hi.astype(F32)).astype(BF16)
        wpad[:, 0:HEADS_PER_GROUP] = w_hi.astype(F32)
        wide = jnp.dot(wpad[...].astype(BF16), expand, preferred_element_type=F32)
        wpad[:, 0:HEADS_PER_GROUP] = w_lo.astype(F32)
        wide = wide + jnp.dot(wpad[...].astype(BF16), expand, preferred_element_type=F32)
        attn = attn + wide * o
    hs = hf[...].astype(F32) + hb[...].astype(F32)
    parts = []
    for h in range(M_HEADS):
        p = hs[:, h * M_HEAD_DIM:(h + 1) * M_HEAD_DIM]
        parts.append(p * lax.rsqrt(jnp.mean(p * p, axis=-1, keepdims=True) + EPS))
    hn = jnp.concatenate(parts, axis=-1) * gn[...]
    ml = jax.nn.sigmoid(mo[...].astype(F32)) * hn
    gates = jax.nn.sigmoid(bg[...].astype(F32))
    merged = (gates[:, :D_MODEL] * jnp.dot(attn.astype(BF16), wa[...], preferred_element_type=F32)
              + gates[:, D_MODEL:] * jnp.dot(ml.astype(BF16), wm[...], preferred_element_type=F32))
    mix = jnp.dot(merged.astype(BF16), wo[...], preferred_element_type=F32)
    y_ref[...] = x_ref[...] + _rms(mix, gpost[...])


def _mix_out(outs, lses, hf, hb, mo, bg, x, p, tm):
    t = x.shape[0]
    row = lambda w: pl.BlockSpec((tm, w), lambda i: (i, 0))
    view = lambda d: pl.BlockSpec((tm // d, d * ATTN_W), lambda i: (i, 0))
    return pl.pallas_call(
        functools.partial(_mix_out_kernel, tm=tm),
        grid=(t // tm,),
        in_specs=[row(ATTN_W), view(4), view(16),
                  row(HEADS_PER_GROUP), row(HEADS_PER_GROUP), row(HEADS_PER_GROUP),
                  row(M_WIDTH), row(M_WIDTH), row(M_WIDTH), row(2 * D_MODEL), row(D_MODEL),
                  _resident((ATTN_W, D_MODEL)), _resident((M_WIDTH, D_MODEL)), _resident((D_MODEL, D_MODEL)),
                  _resident((1, M_WIDTH)), _resident((1, D_MODEL))],
        out_specs=row(D_MODEL),
        out_shape=jax.ShapeDtypeStruct((t, D_MODEL), F32),
        scratch_shapes=[pltpu.VMEM((tm, LSE_PAD), F32),
                        pltpu.VMEM((ATTN_W // LANES, tm, LANES), F32),
                        pltpu.VMEM((ATTN_W // LANES, tm, LANES), F32)],
        compiler_params=_params("parallel"),
        name="mix_out",
    )(*outs, *lses, hf, hb, mo, bg, x, p["w_attn_proj"], p["w_mlstm_proj"], p["w_out"],
      p["g_mlstm_norm"], p["norm_mix_post"])


FFN_CHUNK = 1408


def _ffn_kernel(x_ref, gpre, win, wout, gpost, y_ref):
    x = x_ref[...]
    xb = _rms(x, gpre[...]).astype(BF16)
    acc = jnp.zeros(x.shape, F32)
    for c in range(FFN_HIDDEN // FFN_CHUNK):
        lo = c * FFN_CHUNK
        gate = jnp.dot(xb, win[:, lo:lo + FFN_CHUNK], preferred_element_type=F32)
        up = jnp.dot(xb, win[:, FFN_HIDDEN + lo:FFN_HIDDEN + lo + FFN_CHUNK], preferred_element_type=F32)
        act = (gate * jax.nn.sigmoid(gate) * up).astype(BF16)
        acc = acc + jnp.dot(act, wout[lo:lo + FFN_CHUNK, :], preferred_element_type=F32)
    y_ref[...] = x + _rms(acc, gpost[...])


def _ffn(x, p, tm):
    t = x.shape[0]
    row = pl.BlockSpec((tm, D_MODEL), lambda i: (i, 0))
    return pl.pallas_call(
        _ffn_kernel,
        grid=(t // tm,),
        in_specs=[row, _resident((1, D_MODEL)), _resident((D_MODEL, 2 * FFN_HIDDEN)),
                  _resident((FFN_HIDDEN, D_MODEL)), _resident((1, D_MODEL))],
        out_specs=row,
        out_shape=jax.ShapeDtypeStruct((t, D_MODEL), F32),
        compiler_params=_params("parallel"),
        name="ffn",
    )(x, p["norm_ffn_pre"], p["w_ffn_in"], p["w_ffn_out"], p["norm_ffn_post"])


TM_PROJ = 256
TM_FFN = 512


def _layer(x, nb, s, p):
    qk0, qk1, qk2, v0, v1, v2, mqk, mv, mo, bg, gates = _inproj(x, p, s, TM_PROJ)
    attn = [_attention_group(qk, v, g, nb, s) for g, (qk, v) in enumerate(((qk0, v0), (qk1, v1), (qk2, v2)))]
    hf, hb = _mlstm(mqk, mv, gates, nb, s)
    lses = [a[1].transpose(1, 0, 2).reshape(x.shape[0], HEADS_PER_GROUP) for a in attn]
    x = _mix_out([a[0] for a in attn], lses, hf, hb, mo, bg, x, p, TM_PROJ)
    return _ffn(x, p, TM_FFN)


def _prepare(norm_mix_pre, norm_mix_post, norm_ffn_pre, norm_ffn_post, w_in, b_mlstm_gates, w_conv, b_conv,
             g_mlstm_norm, w_attn_proj, w_mlstm_proj, w_out, w_ffn_in, w_ffn_out):
    depth = w_in.shape[0]
    sizes = (3 * ATTN_W, 3 * ATTN_W, ATTN_W, 2 * M_WIDTH, M_WIDTH, M_WIDTH, 4 * M_HEADS, 2 * D_MODEL)
    offs = [int(c) for c in np.cumsum((0,) + sizes)]
    aq, ak, av, mqk, mv, mo, mg, bg = (w_in[:, :, offs[j]:offs[j + 1]] for j in range(8))
    aq = aq * (HEAD_DIM ** -0.5)
    qk = [jnp.concatenate([aq[:, :, g * ATTN_W:(g + 1) * ATTN_W], ak[:, :, g * ATTN_W:(g + 1) * ATTN_W]], axis=-1)
          for g in range(N_GROUPS)]
    w_main = jnp.concatenate(qk + [av, mqk, mv, mo, bg], axis=-1).astype(BF16)
    wg = mg.reshape(depth, D_MODEL, 4, M_HEADS).transpose(0, 3, 2, 1)
    wg = jnp.pad(wg, ((0, 0), (0, 0), (0, SUBLANES - 4), (0, 0))).reshape(depth, GATE_ROWS, D_MODEL).astype(BF16)
    bgate = b_mlstm_gates.reshape(depth, 4, M_HEADS).transpose(0, 2, 1)
    bgate = jnp.pad(bgate, ((0, 0), (0, 0), (0, SUBLANES - 4))).reshape(depth, GATE_ROWS, 1)
    layers = []
    for l in range(depth):
        layers.append(dict(
            norm_mix_pre=norm_mix_pre[l][None], norm_mix_post=norm_mix_post[l][None],
            norm_ffn_pre=norm_ffn_pre[l][None], norm_ffn_post=norm_ffn_post[l][None],
            w_main=w_main[l], w_gate=wg[l], b_gate=bgate[l],
            w_conv=w_conv[l], b_conv=b_conv[l][None], g_mlstm_norm=g_mlstm_norm[l][None],
            w_attn_proj=w_attn_proj[l].astype(BF16), w_mlstm_proj=w_mlstm_proj[l].astype(BF16),
            w_out=w_out[l].astype(BF16), w_ffn_in=w_ffn_in[l].astype(BF16), w_ffn_out=w_ffn_out[l].astype(BF16)))
    return layers


def _trunk(x, layers):
    nb, s, d = x.shape
    y = x.reshape(nb * s, d)
    for p in layers:
        y = _layer(y, nb, s, p)
    return y.reshape(nb, s, d)


def kernel(x_prompt, x_sample, norm_mix_pre, norm_mix_post, norm_ffn_pre, norm_ffn_post, w_in, b_mlstm_gates,
           w_conv, b_conv, g_mlstm_norm, w_attn_proj, w_mlstm_proj, w_out, w_ffn_in, w_ffn_out):
    layers = _prepare(norm_mix_pre, norm_mix_post, norm_ffn_pre, norm_ffn_post, w_in, b_mlstm_gates, w_conv,
                      b_conv, g_mlstm_norm, w_attn_proj, w_mlstm_proj, w_out, w_ffn_in, w_ffn_out)
    return _trunk(x_prompt, layers), _trunk(x_sample, layers)
```

```python
import functools

import numpy as np
import jax
import jax.numpy as jnp
from jax import lax
from jax.experimental import pallas as pl
from jax.experimental.pallas import tpu as pltpu

F32 = jnp.float32
BF16 = jnp.bfloat16

D_MODEL = 1024
ATTN_GROUPS = ((128, 1), (512, 4), (2048, 16))
DILATIONS = tuple(d for _, d in ATTN_GROUPS)
N_GROUPS = 3
HEADS_PER_GROUP = 8
HEAD_DIM = 64
HALF_WIN = 64
ATTN_W = HEADS_PER_GROUP * HEAD_DIM
M_WIDTH = 1024
M_HEADS = 4
M_HEAD_DIM = 256
M_CHUNK = 128
CONV_WIDTH = 5
FFN_HIDDEN = 2816
EPS = 1e-6
NEG = -1e30
SUBLANES = 8
GATE_ROWS = M_HEADS * SUBLANES

OFF_QK = 0
OFF_V = OFF_QK + 2 * N_GROUPS * ATTN_W
OFF_MQK = OFF_V + ATTN_W
OFF_MV = OFF_MQK + 2 * M_WIDTH
OFF_MO = OFF_MV + M_WIDTH
OFF_BG = OFF_MO + M_WIDTH
N_MAIN = OFF_BG + 2 * D_MODEL

VMEM_LIMIT = 56 * 1024 * 1024


def _alibi_slopes():
    h = np.arange(1, N_GROUPS * HEADS_PER_GROUP + 1, dtype=np.float32)
    return (2.0 ** (-8.0 * h / (N_GROUPS * HEADS_PER_GROUP))).astype(np.float32).reshape(N_GROUPS, HEADS_PER_GROUP)


def _rms(x, gain):
    return x * lax.rsqrt(jnp.mean(x * x, axis=-1, keepdims=True) + EPS) * gain


def _resident(shape):
    nd = len(shape)
    return pl.BlockSpec(shape, lambda *_: (0,) * nd, pipeline_mode=pl.Buffered(1))


_NT = (((1,), (1,)), ((), ()))
LANES = 128


def _slab_store(ref, row0, value):
    rows = value.shape[0]
    for j in range(ref.shape[0]):
        ref[j, row0:row0 + rows, :] = value[:, j * LANES:(j + 1) * LANES]


def _slab_rows(ref, start, size, stride):
    return jnp.concatenate([ref[j, pl.ds(start, size, stride=stride), :] for j in range(ref.shape[0])], axis=1)


def _params(*sem):
    return pltpu.CompilerParams(dimension_semantics=sem, vmem_limit_bytes=VMEM_LIMIT)


def _inproj_kernel(x_ref, xp_ref, xn_ref, gain_ref, w_ref, wg_ref, bias_ref, wc_ref, bc_ref,
                   qk0, qk1, qk2, v0, v1, v2, mqk, mv, mo, bg, gates, mkt, mvt, xs, vs, *, tm, tiles_per_seq):
    i = pl.program_id(0)
    gain = gain_ref[...]
    first = (i % tiles_per_seq) == 0
    last = (i % tiles_per_seq) == tiles_per_seq - 1
    xn = _rms(x_ref[...], gain)
    x_prev = jnp.where(first, 0.0, _rms(xp_ref[...], gain))
    x_next = jnp.where(last, 0.0, _rms(xn_ref[...], gain))
    _slab_store(xs, 0, xn)
    xb = xn.astype(BF16)

    def proj(lhs, a, b):
        return jnp.dot(lhs, w_ref[:, a:b], preferred_element_type=F32)

    qk0[...] = proj(xb, OFF_QK, OFF_QK + 2 * ATTN_W).astype(BF16)
    for g, ref in ((1, qk1), (2, qk2)):
        d = DILATIONS[g]
        n = tm // d
        xperm = jnp.concatenate([_slab_rows(xs, c, n, d) for c in range(d)], axis=0)
        pg = proj(xperm.astype(BF16), OFF_QK + g * 2 * ATTN_W, OFF_QK + (g + 1) * 2 * ATTN_W).astype(BF16)
        for c in range(d):
            ref[:, c * 2 * ATTN_W:(c + 1) * 2 * ATTN_W] = pg[c * n:(c + 1) * n, :]
    vf = proj(xb, OFF_V, OFF_V + ATTN_W)
    v0[...] = vf.astype(BF16)
    _slab_store(vs, 0, vf)
    for g, ref in ((1, v1), (2, v2)):
        d = DILATIONS[g]
        for c in range(d):
            ref[:, c * ATTN_W:(c + 1) * ATTN_W] = _slab_rows(vs, c, tm // d, d).astype(BF16)
    xe = jnp.concatenate([x_prev, xn, x_next], axis=0).astype(BF16)
    ext = tm + 2 * SUBLANES
    cw = ATTN_W
    for cb in range(2 * M_WIDTH // cw):
        pm = proj(xe, OFF_MQK + cb * cw, OFF_MQK + (cb + 1) * cw)
        acc = bc_ref[:, cb * cw:(cb + 1) * cw]
        for tap in range(CONV_WIDTH):
            shift = (CONV_WIDTH // 2 - tap) % ext
            shifted = pm if shift == 0 else pltpu.roll(pm, shift, 0)
            acc = acc + shifted[SUBLANES:SUBLANES + tm, :] * wc_ref[tap:tap + 1, cb * cw:(cb + 1) * cw]
        y = acc * jax.nn.sigmoid(acc)
        if cb < M_WIDTH // cw:
            y = y * (M_HEAD_DIM ** -0.5)
        mqk[:, cb * cw:(cb + 1) * cw] = y.astype(BF16)
        if cb >= M_WIDTH // cw:
            r0 = cb * cw - M_WIDTH
            mkt[r0:r0 + cw, :] = y.T.astype(BF16)
    vm = proj(xb, OFF_MV, OFF_MV + M_WIDTH)
    mv[...] = vm.astype(BF16)
    mvt[...] = vm.T.astype(BF16)
    mo[...] = proj(xb, OFF_MO, OFF_MO + M_WIDTH).astype(BF16)
    bg[...] = proj(xb, OFF_BG, OFF_BG + 2 * D_MODEL).astype(BF16)
    gp = lax.dot_general(wg_ref[...], xb, (((1,), (1,)), ((), ())), preferred_element_type=F32) + bias_ref[...]
    row = lax.broadcasted_iota(jnp.int32, gp.shape, 0)
    log_sig = jnp.minimum(gp, 0.0) - jnp.log1p(jnp.exp(-jnp.abs(gp)))
    gates[...] = jnp.where(row % 2 == 1, log_sig, gp)


def _inproj(x, p, s, tm):
    t = x.shape[0]
    hb = tm // SUBLANES
    n_halo = t // SUBLANES
    row = lambda w: pl.BlockSpec((tm, w), lambda i: (i, 0))
    view = lambda d, w: pl.BlockSpec((tm // d, d * w), lambda i: (i, 0))
    out_specs = (row(2 * ATTN_W), view(4, 2 * ATTN_W), view(16, 2 * ATTN_W),
                 row(ATTN_W), view(4, ATTN_W), view(16, ATTN_W),
                 row(2 * M_WIDTH), row(M_WIDTH), row(M_WIDTH), row(2 * D_MODEL),
                 pl.BlockSpec((GATE_ROWS, tm), lambda i: (0, i)),
                 pl.BlockSpec((M_WIDTH, tm), lambda i: (0, i)), pl.BlockSpec((M_WIDTH, tm), lambda i: (0, i)))
    bshape = lambda d, w: jax.ShapeDtypeStruct((t // d, d * w), BF16)
    out_shape = (bshape(1, 2 * ATTN_W), bshape(4, 2 * ATTN_W), bshape(16, 2 * ATTN_W),
                 bshape(1, ATTN_W), bshape(4, ATTN_W), bshape(16, ATTN_W),
                 bshape(1, 2 * M_WIDTH), bshape(1, M_WIDTH), bshape(1, M_WIDTH), bshape(1, 2 * D_MODEL),
                 jax.ShapeDtypeStruct((GATE_ROWS, t), F32),
                 jax.ShapeDtypeStruct((M_WIDTH, t), BF16), jax.ShapeDtypeStruct((M_WIDTH, t), BF16))
    return pl.pallas_call(
        functools.partial(_inproj_kernel, tm=tm, tiles_per_seq=s // tm),
        grid=(t // tm,),
        in_specs=[row(D_MODEL),
                  pl.BlockSpec((SUBLANES, D_MODEL), lambda i: (jnp.maximum(i * hb - 1, 0), 0)),
                  pl.BlockSpec((SUBLANES, D_MODEL), lambda i: (jnp.minimum((i + 1) * hb, n_halo - 1), 0)),
                  _resident((1, D_MODEL)), _resident((D_MODEL, N_MAIN)), _resident((GATE_ROWS, D_MODEL)),
                  _resident((GATE_ROWS, 1)), _resident((CONV_WIDTH, 2 * M_WIDTH)), _resident((1, 2 * M_WIDTH))],
        out_specs=out_specs,
        out_shape=out_shape,
        scratch_shapes=[pltpu.VMEM((D_MODEL // LANES, tm, LANES), F32), pltpu.VMEM((ATTN_W // LANES, tm, LANES), F32)],
        compiler_params=_params("parallel"),
        name="inproj",
    )(x, x, x, p["norm_mix_pre"], p["w_main"], p["w_gate"], p["b_gate"], p["w_conv"], p["b_conv"])


ATTN_TILE = 128
ATTN_KEYS = ATTN_TILE + 2 * HALF_WIN
PAIR_W = 2 * HEAD_DIM


def _attn_init(bias, vw, *, slopes, dilation):
    qi = lax.broadcasted_iota(jnp.int32, (ATTN_TILE, ATTN_KEYS), 0)
    kj = lax.broadcasted_iota(jnp.int32, (ATTN_TILE, ATTN_KEYS), 1)
    step = jnp.abs(kj - HALF_WIN - qi)
    dist = (step * dilation).astype(F32)
    band = jnp.where(step <= HALF_WIN, 0.0, NEG)
    for variant in range(4):
        mask = band
        if variant & 1:
            mask = jnp.where(kj < HALF_WIN, NEG, mask)
        if variant & 2:
            mask = jnp.where(kj >= ATTN_TILE + HALF_WIN, NEG, mask)
        for h in range(HEADS_PER_GROUP):
            bias[variant, h] = -float(slopes[h]) * dist + mask
    ones = jnp.ones((vw.shape[0], PAIR_W), BF16)
    for pair in range(HEADS_PER_GROUP // 2):
        vw[:, (2 * pair + 1) * PAIR_W:(2 * pair + 2) * PAIR_W] = ones


def _attn_kernel(q_ref, k_ref, kp_ref, kn_ref, v_ref, vp_ref, vn_ref, o_ref, lse_ref, kw, vw, bias,
                 *, slopes, dilation, tqb, nq):
    i = pl.program_id(2)

    @pl.when((pl.program_id(0) == 0) & (pl.program_id(1) == 0) & (i == 0))
    def _():
        _attn_init(bias, vw, slopes=slopes, dilation=dilation)

    kw[0:HALF_WIN, :] = kp_ref[...]
    kw[HALF_WIN:HALF_WIN + tqb, :] = k_ref[...]
    kw[HALF_WIN + tqb:, :] = kn_ref[...]
    for pair in range(HEADS_PER_GROUP // 2):
        src = slice(pair * PAIR_W, (pair + 1) * PAIR_W)
        dst = slice(2 * pair * PAIR_W, (2 * pair + 1) * PAIR_W)
        vw[0:HALF_WIN, dst] = vp_ref[:, src]
        vw[HALF_WIN:HALF_WIN + tqb, dst] = v_ref[:, src]
        vw[HALF_WIN + tqb:, dst] = vn_ref[:, src]

    lane = lax.broadcasted_iota(jnp.int32, (ATTN_TILE, PAIR_W), 1)
    low = lane < HEAD_DIM
    n_tiles = tqb // ATTN_TILE
    for t in range(n_tiles):
        variant = 0
        if t == 0:
            variant = variant + jnp.where(i == 0, 1, 0)
        if t == n_tiles - 1:
            variant = variant + jnp.where(i == nq - 1, 2, 0)
        rows = slice(t * ATTN_TILE, (t + 1) * ATTN_TILE)
        krows = slice(t * ATTN_TILE, t * ATTN_TILE + ATTN_KEYS)
        heads = range(HEADS_PER_GROUP)
        pcols = [slice((h // 2) * PAIR_W, (h // 2 + 1) * PAIR_W) for h in heads]
        scores = []
        for h in heads:
            q2 = q_ref[rows, pcols[h]]
            qh = jnp.where(low if h % 2 == 0 else ~low, q2, jnp.zeros_like(q2))
            scores.append(lax.dot_general(qh, kw[krows, pcols[h]], _NT, preferred_element_type=F32))
        probs, maxes = [], []
        for h in heads:
            s = scores[h] + bias[variant, h]
            m = jnp.max(s, axis=-1, keepdims=True)
            probs.append(jnp.exp(s - m).astype(BF16))
            maxes.append(m)
        pvs = [jnp.dot(probs[h], vw[krows, (h // 2) * 2 * PAIR_W:(h // 2 + 1) * 2 * PAIR_W],
                       preferred_element_type=F32) for h in heads]
        lse = jnp.zeros((ATTN_TILE, PAIR_W), F32)
        outs = []
        for h in heads:
            l = pvs[h][:, PAIR_W:]
            outs.append(pvs[h][:, :PAIR_W] * (1.0 / l))
            lse = jnp.where(lane == h, maxes[h] + jnp.log(l), lse)
        lse_ref[rows, :] = lse[:, 0:HEADS_PER_GROUP]
        for pair in range(HEADS_PER_GROUP // 2):
            o_ref[rows, pcols[2 * pair]] = jnp.where(low, outs[2 * pair], outs[2 * pair + 1]).astype(o_ref.dtype)


def _attention_group(qk, v, g, nb, s):
    dilation = DILATIONS[g]
    rows = qk.shape[0]
    sub_len = s // dilation
    tqb = min(512, sub_len)
    nq = sub_len // tqb
    hb = tqb // HALF_WIN
    n_halo = rows // HALF_WIN

    def spec(size, row_fn, col_fn):
        return pl.BlockSpec((size, ATTN_W), lambda b, c, i: (row_fn(b * nq + i), col_fn(c)))

    cur = lambda r: r
    prv = lambda r: jnp.maximum(r * hb - 1, 0)
    nxt = lambda r: jnp.minimum((r + 1) * hb, n_halo - 1)
    qcol, kcol, vcol = (lambda c: 2 * c), (lambda c: 2 * c + 1), (lambda c: c)
    kern = functools.partial(_attn_kernel, slopes=tuple(_alibi_slopes()[g]), dilation=dilation, tqb=tqb, nq=nq)
    return pl.pallas_call(
        kern,
        grid=(nb, dilation, nq),
        in_specs=[spec(tqb, cur, qcol), spec(tqb, cur, kcol), spec(HALF_WIN, prv, kcol), spec(HALF_WIN, nxt, kcol),
                  spec(tqb, cur, vcol), spec(HALF_WIN, prv, vcol), spec(HALF_WIN, nxt, vcol)],
        out_specs=(spec(tqb, cur, vcol),
                   pl.BlockSpec((None, tqb, HEADS_PER_GROUP), lambda b, c, i: (c, b * nq + i, 0))),
        out_shape=(jax.ShapeDtypeStruct((rows, dilation * ATTN_W), BF16),
                   jax.ShapeDtypeStruct((dilation, rows, HEADS_PER_GROUP), F32)),
        scratch_shapes=[pltpu.VMEM((tqb + 2 * HALF_WIN, ATTN_W), BF16),
                        pltpu.VMEM((tqb + 2 * HALF_WIN, 2 * ATTN_W), BF16),
                        pltpu.VMEM((4, HEADS_PER_GROUP, ATTN_TILE, ATTN_KEYS), F32)],
        compiler_params=_params("arbitrary", "arbitrary", "arbitrary"),
        name=f"attn_g{g}",
    )(qk, qk, qk, qk, v, v, v)


MLSTM_CHUNKS = 4


def _mlstm_stage1a(q, k, li_row, lf_row, tri):
    return dict(qk=lax.dot_general(q, k, _NT, preferred_element_type=F32), li_row=li_row,
                cum_col=jnp.sum(jnp.where(tri, lf_row, 0.0), axis=1, keepdims=True),
                tot=jnp.sum(lf_row, axis=1, keepdims=True))


def _mlstm_stage1b(rec, tri, eye):
    cum_row = jnp.sum(jnp.where(eye, rec["cum_col"], 0.0), axis=0, keepdims=True)
    li_row = rec.pop("li_row")
    dmat = jnp.where(tri, rec["cum_col"] - cum_row + li_row, NEG)
    a = rec["tot"] - cum_row + li_row
    rec.update(dmat=dmat, a=a, m_loc=jnp.max(dmat, axis=1, keepdims=True), a_max=jnp.max(a, axis=1, keepdims=True))


def _mlstm_stage1c(rec):
    rec.update(p=jnp.exp(rec.pop("dmat") - rec["m_loc"]), w_k=jnp.exp(rec.pop("a") - rec["a_max"]))


def _mlstm_stage2(rec, kt):
    sc = rec["qk"] * rec["p"]
    w_k = rec["w_k"]
    rec.update(sc=sc.astype(BF16), den=jnp.sum(sc, axis=1, keepdims=True),
               kwt=(kt.astype(F32) * w_k).astype(BF16),
               w_rows=jnp.broadcast_to(w_k, (SUBLANES, w_k.shape[1])).astype(BF16))


def _mlstm_stage3(rec, k, v, vt):
    rec.update(num=jnp.dot(rec["sc"], v, preferred_element_type=F32),
               upd=lax.dot_general(rec["kwt"], vt, _NT, preferred_element_type=F32),
               n_upd=jnp.dot(rec["w_rows"], k, preferred_element_type=F32)[0:1, :])


def _mlstm_stage4a(rec, m_prev):
    m_inter = rec["cum_col"] + m_prev
    m_t = jnp.maximum(m_inter, rec["m_loc"])
    m_new = jnp.maximum(rec["tot"] + m_prev, rec["a_max"])
    rec.update(r_loc=jnp.exp(rec["m_loc"] - m_t), w_inter=jnp.exp(m_inter - m_t), floor=jnp.exp(-m_t),
               alpha=jnp.exp(rec["tot"] + m_prev - m_new), beta=jnp.exp(rec["a_max"] - m_new))
    return m_new


def _mlstm_stage4b(rec, q, state):
    c_prev, n_prev = state
    qc = jnp.dot(q, c_prev.astype(BF16), preferred_element_type=F32)
    qn = jnp.sum(q.astype(F32) * n_prev, axis=1, keepdims=True)
    r_loc, w_inter, alpha, beta = rec["r_loc"], rec["w_inter"], rec["alpha"], rec["beta"]
    den = r_loc * rec["den"] + w_inter * qn
    h = (r_loc * rec["num"] + w_inter * qc) * (1.0 / jnp.maximum(jnp.abs(den), rec["floor"]))
    return h, (alpha * c_prev + beta * rec["upd"], alpha * n_prev + beta * rec["n_upd"])


def _mlstm_kernel(qf, kf, vf, ktf, vtf, gf, qb, kb, vb, ktb, vtb, gb, hf_ref, hb_ref, c_scr, n_scr, m_scr):
    @pl.when(pl.program_id(2) == 0)
    def _():
        c_scr[...] = jnp.zeros_like(c_scr)
        n_scr[...] = jnp.zeros_like(n_scr)
        m_scr[...] = jnp.zeros_like(m_scr)

    L = M_CHUNK
    ti = lax.broadcasted_iota(jnp.int32, (L, L), 0)
    si = lax.broadcasted_iota(jnp.int32, (L, L), 1)
    eye = ti == si
    f_rows = [slice(c * L, (c + 1) * L) for c in range(MLSTM_CHUNKS)]
    dirs = (((qf, kf, vf, ktf, vtf, gf), (0, 1), si <= ti, f_rows, hf_ref),
            ((qb, kb, vb, ktb, vtb, gb), (2, 3), si >= ti, f_rows[::-1], hb_ref))
    recs = [[], []]
    for c in range(MLSTM_CHUNKS):
        for d, ((q, k, _, _, _, g), (gi, gf_), tri, rows, _) in enumerate(dirs):
            r = rows[c]
            recs[d].append(_mlstm_stage1a(q[r, :], k[r, :], g[gi:gi + 1, r], g[gf_:gf_ + 1, r], tri))
    for c in range(MLSTM_CHUNKS):
        for d in range(2):
            _mlstm_stage1b(recs[d][c], dirs[d][2], eye)
    for c in range(MLSTM_CHUNKS):
        for d in range(2):
            _mlstm_stage1c(recs[d][c])
    for c in range(MLSTM_CHUNKS):
        for d, ((_, _, _, kt, _, _), _, _, rows, _) in enumerate(dirs):
            _mlstm_stage2(recs[d][c], kt[:, rows[c]])
    for c in range(MLSTM_CHUNKS):
        for d, ((_, k, v, _, vt, _), _, _, rows, _) in enumerate(dirs):
            _mlstm_stage3(recs[d][c], k[rows[c], :], v[rows[c], :], vt[:, rows[c]])
    m = [m_scr[0], m_scr[1]]
    for c in range(MLSTM_CHUNKS):
        for d in range(2):
            m[d] = _mlstm_stage4a(recs[d][c], m[d])
    state = [(c_scr[0], n_scr[0]), (c_scr[1], n_scr[1])]
    for c in range(MLSTM_CHUNKS):
        for d, ((q, _, _, _, _, _), _, _, rows, h_ref) in enumerate(dirs):
            h, state[d] = _mlstm_stage4b(recs[d][c], q[rows[c], :], state[d])
            h_ref[rows[c], :] = h.astype(h_ref.dtype)
    for d in range(2):
        c_scr[d] = state[d][0]
        n_scr[d] = state[d][1]
        m_scr[d] = m[d]


def _mlstm(qk, v, kt, vt, gates, nb, s):
    t = v.shape[0]
    rows = MLSTM_CHUNKS * M_CHUNK
    nr = s // rows
    fblk = lambda b, j: b * nr + j
    bblk = lambda b, j: b * nr + (nr - 1 - j)

    def specs(blk):
        return [pl.BlockSpec((rows, M_HEAD_DIM), lambda b, h, j: (blk(b, j), h)),
                pl.BlockSpec((rows, M_HEAD_DIM), lambda b, h, j: (blk(b, j), M_HEADS + h)),
                pl.BlockSpec((rows, M_HEAD_DIM), lambda b, h, j: (blk(b, j), h)),
                pl.BlockSpec((M_HEAD_DIM, rows), lambda b, h, j: (h, blk(b, j))),
                pl.BlockSpec((M_HEAD_DIM, rows), lambda b, h, j: (h, blk(b, j))),
                pl.BlockSpec((SUBLANES, rows), lambda b, h, j: (h, blk(b, j)))]

    out = lambda blk: pl.BlockSpec((rows, M_HEAD_DIM), lambda b, h, j: (blk(b, j), h))
    return pl.pallas_call(
        _mlstm_kernel,
        grid=(nb, M_HEADS, nr),
        in_specs=specs(fblk) + specs(bblk),
        out_specs=(out(fblk), out(bblk)),
        out_shape=(jax.ShapeDtypeStruct((t, M_WIDTH), BF16), jax.ShapeDtypeStruct((t, M_WIDTH), BF16)),
        scratch_shapes=[pltpu.VMEM((2, M_HEAD_DIM, M_HEAD_DIM), F32), pltpu.VMEM((2, 1, M_HEAD_DIM), F32),
                        pltpu.VMEM((2, 1, 1), F32)],
        compiler_params=_params("parallel", "parallel", "arbitrary"),
        name="mlstm",
    )(qk, qk, v, kt, vt, gates, qk, qk, v, kt, vt, gates)


LSE_PAD = 128


def _expand_matrix():
    r = np.arange(LSE_PAD)[:, None]
    c = np.arange(N_GROUPS * ATTN_W)[None, :]
    hit = (r < N_GROUPS * 2 * HEADS_PER_GROUP) & (c // ATTN_W == r // (2 * HEADS_PER_GROUP)) \
        & ((c % ATTN_W) // HEAD_DIM == r % HEADS_PER_GROUP)
    return jnp.asarray(hit, dtype=BF16)


def _mix_out_kernel(o0, o1, o2, l0, l1, l2, hf, hb, mo, bg, x_ref, wa, wm, wo, gn, gpost, expand, y_ref,
                    wpad, os1, os2, *, tm):
    hs = hf[...].astype(F32) + hb[...].astype(F32)
    parts = []
    for h in range(M_HEADS):
        p = hs[:, h * M_HEAD_DIM:(h + 1) * M_HEAD_DIM]
        parts.append(p * lax.rsqrt(jnp.mean(p * p, axis=-1, keepdims=True) + EPS))
    hn = jnp.concatenate(parts, axis=-1) * gn[...]
    ml = jax.nn.sigmoid(mo[...].astype(F32)) * hn
    ml_proj = jnp.dot(ml.astype(BF16), wm[...], preferred_element_type=F32)
    lses = [l0[...], l1[...], l2[...]]
    mx = jnp.maximum(jnp.maximum(lses[0], lses[1]), lses[2])
    es = [jnp.exp(l - mx) for l in lses]
    inv_z = 1.0 / (es[0] + es[1] + es[2])
    wpad[...] = jnp.zeros_like(wpad)
    for g, e in enumerate(es):
        w = e * inv_z
        w_hi = w.astype(BF16).astype(F32)
        base = g * 2 * HEADS_PER_GROUP
        wpad[:, base:base + HEADS_PER_GROUP] = w_hi
        wpad[:, base + HEADS_PER_GROUP:base + 2 * HEADS_PER_GROUP] = w - w_hi
    wide = jnp.dot(wpad[...].astype(BF16), expand[...], preferred_element_type=F32)
    for o, osc, d in ((o1, os1, DILATIONS[1]), (o2, os2, DILATIONS[2])):
        n = tm // d
        for c in range(d):
            for j in range(ATTN_W // LANES):
                lo = c * ATTN_W + j * LANES
                osc[j, pl.ds(c, n, stride=d), :] = o[:, lo:lo + LANES].astype(F32)
    outs = [o0[...].astype(F32)] + [jnp.concatenate([osc[j] for j in range(ATTN_W // LANES)], axis=1)
                                    for osc in (os1, os2)]
    attn = (wide[:, :ATTN_W] * outs[0] + wide[:, ATTN_W:2 * ATTN_W] * outs[1] + wide[:, 2 * ATTN_W:] * outs[2])
    attn_proj = jnp.dot(attn.astype(BF16), wa[...], preferred_element_type=F32)
    gates = jax.nn.sigmoid(bg[...].astype(F32))
    merged = gates[:, :D_MODEL] * attn_proj + gates[:, D_MODEL:] * ml_proj
    mix = jnp.dot(merged.astype(BF16), wo[...], preferred_element_type=F32)
    y_ref[...] = x_ref[...] + _rms(mix, gpost[...])


def _mix_out(outs, lses, hf, hb, mo, bg, x, p, tm):
    t = x.shape[0]
    row = lambda w: pl.BlockSpec((tm, w), lambda i: (i, 0))
    view = lambda d: pl.BlockSpec((tm // d, d * ATTN_W), lambda i: (i, 0))
    return pl.pallas_call(
        functools.partial(_mix_out_kernel, tm=tm),
        grid=(t // tm,),
        in_specs=[row(ATTN_W), view(4), view(16),
                  row(HEADS_PER_GROUP), row(HEADS_PER_GROUP), row(HEADS_PER_GROUP),
                  row(M_WIDTH), row(M_WIDTH), row(M_WIDTH), row(2 * D_MODEL), row(D_MODEL),
                  _resident((ATTN_W, D_MODEL)), _resident((M_WIDTH, D_MODEL)), _resident((D_MODEL, D_MODEL)),
                  _resident((1, M_WIDTH)), _resident((1, D_MODEL)), _resident((LSE_PAD, N_GROUPS * ATTN_W))],
        out_specs=row(D_MODEL),
        out_shape=jax.ShapeDtypeStruct((t, D_MODEL), F32),
        scratch_shapes=[pltpu.VMEM((tm, LSE_PAD), F32),
                        pltpu.VMEM((ATTN_W // LANES, tm, LANES), F32),
                        pltpu.VMEM((ATTN_W // LANES, tm, LANES), F32)],
        compiler_params=_params("parallel"),
        name="mix_out",
    )(*outs, *lses, hf, hb, mo, bg, x, p["w_attn_proj"], p["w_mlstm_proj"], p["w_out"],
      p["g_mlstm_norm"], p["norm_mix_post"], _expand_matrix())


FFN_CHUNK = 1408


def _ffn_kernel(x_ref, gpre, win, wout, gpost, y_ref):
    x = x_ref[...]
    xb = _rms(x, gpre[...]).astype(BF16)
    acc = jnp.zeros(x.shape, F32)
    for c in range(FFN_HIDDEN // FFN_CHUNK):
        lo = c * FFN_CHUNK
        gate = jnp.dot(xb, win[:, lo:lo + FFN_CHUNK], preferred_element_type=F32)
        up = jnp.dot(xb, win[:, FFN_HIDDEN + lo:FFN_HIDDEN + lo + FFN_CHUNK], preferred_element_type=F32)
        act = (gate * jax.nn.sigmoid(gate) * up).astype(BF16)
        acc = acc + jnp.dot(act, wout[lo:lo + FFN_CHUNK, :], preferred_element_type=F32)
    y_ref[...] = x + _rms(acc, gpost[...])


def _ffn(x, p, tm):
    t = x.shape[0]
    row = pl.BlockSpec((tm, D_MODEL), lambda i: (i, 0))
    return pl.pallas_call(
        _ffn_kernel,
        grid=(t // tm,),
        in_specs=[row, _resident((1, D_MODEL)), _resident((D_MODEL, 2 * FFN_HIDDEN)),
                  _resident((FFN_HIDDEN, D_MODEL)), _resident((1, D_MODEL))],
        out_specs=row,
        out_shape=jax.ShapeDtypeStruct((t, D_MODEL), F32),
        compiler_params=_params("parallel"),
        name="ffn",
    )(x, p["norm_ffn_pre"], p["w_ffn_in"], p["w_ffn_out"], p["norm_ffn_post"])


TM_PROJ = 256
TM_FFN = 512


def _layer(x, nb, s, p):
    qk0, qk1, qk2, v0, v1, v2, mqk, mv, mo, bg, gates, mkt, mvt = _inproj(x, p, s, TM_PROJ)
    attn = [_attention_group(qk, v, g, nb, s) for g, (qk, v) in enumerate(((qk0, v0), (qk1, v1), (qk2, v2)))]
    hf, hb = _mlstm(mqk, mv, mkt, mvt, gates, nb, s)
    lses = [a[1].transpose(1, 0, 2).reshape(x.shape[0], HEADS_PER_GROUP) for a in attn]
    x = _mix_out([a[0] for a in attn], lses, hf, hb, mo, bg, x, p, TM_PROJ)
    return _ffn(x, p, TM_FFN)


def _prepare(norm_mix_pre, norm_mix_post, norm_ffn_pre, norm_ffn_post, w_in, b_mlstm_gates, w_conv, b_conv,
             g_mlstm_norm, w_attn_proj, w_mlstm_proj, w_out, w_ffn_in, w_ffn_out):
    depth = w_in.shape[0]
    sizes = (3 * ATTN_W, 3 * ATTN_W, ATTN_W, 2 * M_WIDTH, M_WIDTH, M_WIDTH, 4 * M_HEADS, 2 * D_MODEL)
    offs = [int(c) for c in np.cumsum((0,) + sizes)]
    aq, ak, av, mqk, mv, mo, mg, bg = (w_in[:, :, offs[j]:offs[j + 1]] for j in range(8))
    aq = aq * (HEAD_DIM ** -0.5)
    qk = [jnp.concatenate([aq[:, :, g * ATTN_W:(g + 1) * ATTN_W], ak[:, :, g * ATTN_W:(g + 1) * ATTN_W]], axis=-1)
          for g in range(N_GROUPS)]
    w_main = jnp.concatenate(qk + [av, mqk, mv, mo, bg], axis=-1).astype(BF16)
    wg = mg.reshape(depth, D_MODEL, 4, M_HEADS).transpose(0, 3, 2, 1)
    wg = jnp.pad(wg, ((0, 0), (0, 0), (0, SUBLANES - 4), (0, 0))).reshape(depth, GATE_ROWS, D_MODEL).astype(BF16)
    bgate = b_mlstm_gates.reshape(depth, 4, M_HEADS).transpose(0, 2, 1)
    bgate = jnp.pad(bgate, ((0, 0), (0, 0), (0, SUBLANES - 4))).reshape(depth, GATE_ROWS, 1)
    layers = []
    for l in range(depth):
        layers.append(dict(
            norm_mix_pre=norm_mix_pre[l][None], norm_mix_post=norm_mix_post[l][None],
            norm_ffn_pre=norm_ffn_pre[l][None], norm_ffn_post=norm_ffn_post[l][None],
            w_main=w_main[l], w_gate=wg[l], b_gate=bgate[l],
            w_conv=w_conv[l], b_conv=b_conv[l][None], g_mlstm_norm=g_mlstm_norm[l][None],
            w_attn_proj=w_attn_proj[l].astype(BF16), w_mlstm_proj=w_mlstm_proj[l].astype(BF16),
            w_out=w_out[l].astype(BF16), w_ffn_in=w_ffn_in[l].astype(BF16), w_ffn_out=w_ffn_out[l].astype(BF16)))
    return layers


def _trunk(x, layers):
    nb, s, d = x.shape
    y = x.reshape(nb * s, d)
    for p in layers:
        y = _layer(y, nb, s, p)
    return y.reshape(nb, s, d)


def kernel(x_prompt, x_sample, norm_mix_pre, norm_mix_post, norm_ffn_pre, norm_ffn_post, w_in, b_mlstm_gates,
           w_conv, b_conv, g_mlstm_norm, w_attn_proj, w_mlstm_proj, w_out, w_ffn_in, w_ffn_out):
    layers = _prepare(norm_mix_pre, norm_mix_post, norm_ffn_pre, norm_ffn_post, w_in, b_mlstm_gates, w_conv,
                      b_conv, g_mlstm_norm, w_attn_proj, w_mlstm_proj, w_out, w_ffn_in, w_ffn_out)
    return _trunk(x_prompt, layers), _trunk(x_sample, layers)
```

```python
import functools

import numpy as np
import jax
import jax.numpy as jnp
from jax import lax
from jax.experimental import pallas as pl
from jax.experimental.pallas import tpu as pltpu

F32 = jnp.float32
BF16 = jnp.bfloat16

D_MODEL = 1024
ATTN_GROUPS = ((128, 1), (512, 4), (2048, 16))
DILATIONS = tuple(d for _, d in ATTN_GROUPS)
N_GROUPS = 3
HEADS_PER_GROUP = 8
HEAD_DIM = 64
HALF_WIN = 64
ATTN_W = HEADS_PER_GROUP * HEAD_DIM
M_WIDTH = 1024
M_HEADS = 4
M_HEAD_DIM = 256
M_CHUNK = 128
CONV_WIDTH = 5
FFN_HIDDEN = 2816
EPS = 1e-6
NEG = -1e30
SUBLANES = 8
GATE_ROWS = M_HEADS * SUBLANES

OFF_QK = 0
OFF_V = OFF_QK + 2 * N_GROUPS * ATTN_W
OFF_MQK = OFF_V + ATTN_W
OFF_MV = OFF_MQK + 2 * M_WIDTH
OFF_MO = OFF_MV + M_WIDTH
OFF_BG = OFF_MO + M_WIDTH
N_MAIN = OFF_BG + 2 * D_MODEL

VMEM_LIMIT = 56 * 1024 * 1024


def _alibi_slopes():
    h = np.arange(1, N_GROUPS * HEADS_PER_GROUP + 1, dtype=np.float32)
    return (2.0 ** (-8.0 * h / (N_GROUPS * HEADS_PER_GROUP))).astype(np.float32).reshape(N_GROUPS, HEADS_PER_GROUP)


def _rms(x, gain):
    return x * lax.rsqrt(jnp.mean(x * x, axis=-1, keepdims=True) + EPS) * gain


def _resident(shape):
    nd = len(shape)
    return pl.BlockSpec(shape, lambda *_: (0,) * nd, pipeline_mode=pl.Buffered(1))


_NT = (((1,), (1,)), ((), ()))
LANES = 128


def _slab_store(ref, row0, value):
    rows = value.shape[0]
    for j in range(ref.shape[0]):
        ref[j, row0:row0 + rows, :] = value[:, j * LANES:(j + 1) * LANES]


def _slab_rows(ref, start, size, stride):
    return jnp.concatenate([ref[j, pl.ds(start, size, stride=stride), :] for j in range(ref.shape[0])], axis=1)


def _params(*sem):
    return pltpu.CompilerParams(dimension_semantics=sem, vmem_limit_bytes=VMEM_LIMIT)


def _inproj_kernel(x_ref, xp_ref, xn_ref, gain_ref, w_ref, wg_ref, bias_ref, wc_ref, bc_ref,
                   qk0, qk1, qk2, v0, v1, v2, mqk, mv, mo, bg, gates, mkt, mvt, xs, vs, *, tm, tiles_per_seq):
    i = pl.program_id(0)
    gain = gain_ref[...]
    first = (i % tiles_per_seq) == 0
    last = (i % tiles_per_seq) == tiles_per_seq - 1
    xn = _rms(x_ref[...], gain)
    x_prev = jnp.where(first, 0.0, _rms(xp_ref[...], gain))
    x_next = jnp.where(last, 0.0, _rms(xn_ref[...], gain))
    _slab_store(xs, 0, xn)
    xb = xn.astype(BF16)

    def proj(lhs, a, b):
        return jnp.dot(lhs, w_ref[:, a:b], preferred_element_type=F32)

    qk0[...] = proj(xb, OFF_QK, OFF_QK + 2 * ATTN_W).astype(BF16)
    for g, ref in ((1, qk1), (2, qk2)):
        d = DILATIONS[g]
        n = tm // d
        xperm = jnp.concatenate([_slab_rows(xs, c, n, d) for c in range(d)], axis=0)
        pg = proj(xperm.astype(BF16), OFF_QK + g * 2 * ATTN_W, OFF_QK + (g + 1) * 2 * ATTN_W).astype(BF16)
        for c in range(d):
            ref[:, c * 2 * ATTN_W:(c + 1) * 2 * ATTN_W] = pg[c * n:(c + 1) * n, :]
    vf = proj(xb, OFF_V, OFF_V + ATTN_W)
    v0[...] = vf.astype(BF16)
    _slab_store(vs, 0, vf)
    for g, ref in ((1, v1), (2, v2)):
        d = DILATIONS[g]
        for c in range(d):
            ref[:, c * ATTN_W:(c + 1) * ATTN_W] = _slab_rows(vs, c, tm // d, d).astype(BF16)
    xe = jnp.concatenate([x_prev, xn, x_next], axis=0).astype(BF16)
    ext = tm + 2 * SUBLANES
    cw = ATTN_W
    for cb in range(2 * M_WIDTH // cw):
        pm = proj(xe, OFF_MQK + cb * cw, OFF_MQK + (cb + 1) * cw)
        acc = bc_ref[:, cb * cw:(cb + 1) * cw]
        for tap in range(CONV_WIDTH):
            shift = (CONV_WIDTH // 2 - tap) % ext
            shifted = pm if shift == 0 else pltpu.roll(pm, shift, 0)
            acc = acc + shifted[SUBLANES:SUBLANES + tm, :] * wc_ref[tap:tap + 1, cb * cw:(cb + 1) * cw]
        y = acc * jax.nn.sigmoid(acc)
        if cb < M_WIDTH // cw:
            y = y * (M_HEAD_DIM ** -0.5)
        mqk[:, cb * cw:(cb + 1) * cw] = y.astype(BF16)
        if cb >= M_WIDTH // cw:
            r0 = cb * cw - M_WIDTH
            mkt[r0:r0 + cw, :] = y.T.astype(BF16)
    vm = proj(xb, OFF_MV, OFF_MV + M_WIDTH)
    mv[...] = vm.astype(BF16)
    mvt[...] = vm.T.astype(BF16)
    mo[...] = proj(xb, OFF_MO, OFF_MO + M_WIDTH).astype(BF16)
    bg[...] = proj(xb, OFF_BG, OFF_BG + 2 * D_MODEL).astype(BF16)
    gp = lax.dot_general(wg_ref[...], xb, (((1,), (1,)), ((), ())), preferred_element_type=F32) + bias_ref[...]
    row = lax.broadcasted_iota(jnp.int32, gp.shape, 0)
    log_sig = jnp.minimum(gp, 0.0) - jnp.log1p(jnp.exp(-jnp.abs(gp)))
    gates[...] = jnp.where(row % 2 == 1, log_sig, gp)


def _inproj(x, p, s, tm):
    t = x.shape[0]
    hb = tm // SUBLANES
    n_halo = t // SUBLANES
    row = lambda w: pl.BlockSpec((tm, w), lambda i: (i, 0))
    view = lambda d, w: pl.BlockSpec((tm // d, d * w), lambda i: (i, 0))
    out_specs = (row(2 * ATTN_W), view(4, 2 * ATTN_W), view(16, 2 * ATTN_W),
                 row(ATTN_W), view(4, ATTN_W), view(16, ATTN_W),
                 row(2 * M_WIDTH), row(M_WIDTH), row(M_WIDTH), row(2 * D_MODEL),
                 pl.BlockSpec((GATE_ROWS, tm), lambda i: (0, i)),
                 pl.BlockSpec((M_WIDTH, tm), lambda i: (0, i)), pl.BlockSpec((M_WIDTH, tm), lambda i: (0, i)))
    bshape = lambda d, w: jax.ShapeDtypeStruct((t // d, d * w), BF16)
    out_shape = (bshape(1, 2 * ATTN_W), bshape(4, 2 * ATTN_W), bshape(16, 2 * ATTN_W),
                 bshape(1, ATTN_W), bshape(4, ATTN_W), bshape(16, ATTN_W),
                 bshape(1, 2 * M_WIDTH), bshape(1, M_WIDTH), bshape(1, M_WIDTH), bshape(1, 2 * D_MODEL),
                 jax.ShapeDtypeStruct((GATE_ROWS, t), F32),
                 jax.ShapeDtypeStruct((M_WIDTH, t), BF16), jax.ShapeDtypeStruct((M_WIDTH, t), BF16))
    return pl.pallas_call(
        functools.partial(_inproj_kernel, tm=tm, tiles_per_seq=s // tm),
        grid=(t // tm,),
        in_specs=[row(D_MODEL),
                  pl.BlockSpec((SUBLANES, D_MODEL), lambda i: (jnp.maximum(i * hb - 1, 0), 0)),
                  pl.BlockSpec((SUBLANES, D_MODEL), lambda i: (jnp.minimum((i + 1) * hb, n_halo - 1), 0)),
                  _resident((1, D_MODEL)), _resident((D_MODEL, N_MAIN)), _resident((GATE_ROWS, D_MODEL)),
                  _resident((GATE_ROWS, 1)), _resident((CONV_WIDTH, 2 * M_WIDTH)), _resident((1, 2 * M_WIDTH))],
        out_specs=out_specs,
        out_shape=out_shape,
        scratch_shapes=[pltpu.VMEM((D_MODEL // LANES, tm, LANES), F32), pltpu.VMEM((ATTN_W // LANES, tm, LANES), F32)],
        compiler_params=_params("parallel"),
        name="inproj",
    )(x, x, x, p["norm_mix_pre"], p["w_main"], p["w_gate"], p["b_gate"], p["w_conv"], p["b_conv"])


ATTN_TILE = 128
ATTN_KEYS = ATTN_TILE + 2 * HALF_WIN
ATTN_BLOCK = 1024
PAIR_W = 2 * HEAD_DIM


def _attn_init(bias, vw, *, slopes, dilation):
    qi = lax.broadcasted_iota(jnp.int32, (ATTN_TILE, ATTN_KEYS), 0)
    kj = lax.broadcasted_iota(jnp.int32, (ATTN_TILE, ATTN_KEYS), 1)
    step = jnp.abs(kj - HALF_WIN - qi)
    dist = (step * dilation).astype(F32)
    band = jnp.where(step <= HALF_WIN, 0.0, NEG)
    for variant in range(4):
        mask = band
        if variant & 1:
            mask = jnp.where(kj < HALF_WIN, NEG, mask)
        if variant & 2:
            mask = jnp.where(kj >= ATTN_TILE + HALF_WIN, NEG, mask)
        for h in range(HEADS_PER_GROUP):
            bias[variant, h] = -float(slopes[h]) * dist + mask
    ones = jnp.ones((vw.shape[0], PAIR_W), BF16)
    for pair in range(HEADS_PER_GROUP // 2):
        vw[:, (2 * pair + 1) * PAIR_W:(2 * pair + 2) * PAIR_W] = ones


def _attn_kernel(q_ref, k_ref, kp_ref, kn_ref, v_ref, vp_ref, vn_ref, o_ref, lse_ref, kw, vw, bias,
                 *, slopes, dilation, tqb, nq):
    i = pl.program_id(2)

    @pl.when((pl.program_id(0) == 0) & (pl.program_id(1) == 0) & (i == 0))
    def _():
        _attn_init(bias, vw, slopes=slopes, dilation=dilation)

    kw[0:HALF_WIN, :] = kp_ref[...]
    kw[HALF_WIN:HALF_WIN + tqb, :] = k_ref[...]
    kw[HALF_WIN + tqb:, :] = kn_ref[...]
    for pair in range(HEADS_PER_GROUP // 2):
        src = slice(pair * PAIR_W, (pair + 1) * PAIR_W)
        dst = slice(2 * pair * PAIR_W, (2 * pair + 1) * PAIR_W)
        vw[0:HALF_WIN, dst] = vp_ref[:, src]
        vw[HALF_WIN:HALF_WIN + tqb, dst] = v_ref[:, src]
        vw[HALF_WIN + tqb:, dst] = vn_ref[:, src]

    lane = lax.broadcasted_iota(jnp.int32, (ATTN_TILE, PAIR_W), 1)
    low = lane < HEAD_DIM
    n_tiles = tqb // ATTN_TILE
    for t in range(n_tiles):
        variant = 0
        if t == 0:
            variant = variant + jnp.where(i == 0, 1, 0)
        if t == n_tiles - 1:
            variant = variant + jnp.where(i == nq - 1, 2, 0)
        rows = slice(t * ATTN_TILE, (t + 1) * ATTN_TILE)
        krows = slice(t * ATTN_TILE, t * ATTN_TILE + ATTN_KEYS)
        heads = range(HEADS_PER_GROUP)
        pcols = [slice((h // 2) * PAIR_W, (h // 2 + 1) * PAIR_W) for h in heads]
        scores = []
        for h in heads:
            q2 = q_ref[rows, pcols[h]]
            qh = jnp.where(low if h % 2 == 0 else ~low, q2, jnp.zeros_like(q2))
            scores.append(lax.dot_general(qh, kw[krows, pcols[h]], _NT, preferred_element_type=F32))
        probs, maxes = [], []
        for h in heads:
            s = scores[h] + bias[variant, h]
            m = jnp.max(s, axis=-1, keepdims=True)
            probs.append(jnp.exp(s - m).astype(BF16))
            maxes.append(m)
        pvs = [jnp.dot(probs[h], vw[krows, (h // 2) * 2 * PAIR_W:(h // 2 + 1) * 2 * PAIR_W],
                       preferred_element_type=F32) for h in heads]
        lse = jnp.zeros((ATTN_TILE, PAIR_W), F32)
        outs = []
        for h in heads:
            l = pvs[h][:, PAIR_W:]
            outs.append(pvs[h][:, :PAIR_W] * (1.0 / l))
            lse = jnp.where(lane == h, maxes[h] + jnp.log(l), lse)
        lse_ref[rows, :] = lse[:, 0:HEADS_PER_GROUP]
        for pair in range(HEADS_PER_GROUP // 2):
            o_ref[rows, pcols[2 * pair]] = jnp.where(low, outs[2 * pair], outs[2 * pair + 1]).astype(o_ref.dtype)


def _attention_group(qk, v, g, nb, s):
    dilation = DILATIONS[g]
    rows = qk.shape[0]
    sub_len = s // dilation
    tqb = min(ATTN_BLOCK, sub_len)
    nq = sub_len // tqb
    hb = tqb // HALF_WIN
    n_halo = rows // HALF_WIN

    def spec(size, row_fn, col_fn):
        return pl.BlockSpec((size, ATTN_W), lambda b, c, i: (row_fn(b * nq + i), col_fn(c)))

    cur = lambda r: r
    prv = lambda r: jnp.maximum(r * hb - 1, 0)
    nxt = lambda r: jnp.minimum((r + 1) * hb, n_halo - 1)
    qcol, kcol, vcol = (lambda c: 2 * c), (lambda c: 2 * c + 1), (lambda c: c)
    kern = functools.partial(_attn_kernel, slopes=tuple(_alibi_slopes()[g]), dilation=dilation, tqb=tqb, nq=nq)
    return pl.pallas_call(
        kern,
        grid=(nb, dilation, nq),
        in_specs=[spec(tqb, cur, qcol), spec(tqb, cur, kcol), spec(HALF_WIN, prv, kcol), spec(HALF_WIN, nxt, kcol),
                  spec(tqb, cur, vcol), spec(HALF_WIN, prv, vcol), spec(HALF_WIN, nxt, vcol)],
        out_specs=(spec(tqb, cur, vcol),
                   pl.BlockSpec((None, tqb, HEADS_PER_GROUP), lambda b, c, i: (c, b * nq + i, 0))),
        out_shape=(jax.ShapeDtypeStruct((rows, dilation * ATTN_W), BF16),
                   jax.ShapeDtypeStruct((dilation, rows, HEADS_PER_GROUP), F32)),
        scratch_shapes=[pltpu.VMEM((tqb + 2 * HALF_WIN, ATTN_W), BF16),
                        pltpu.VMEM((tqb + 2 * HALF_WIN, 2 * ATTN_W), BF16),
                        pltpu.VMEM((4, HEADS_PER_GROUP, ATTN_TILE, ATTN_KEYS), F32)],
        compiler_params=_params("arbitrary", "arbitrary", "arbitrary"),
        name=f"attn_g{g}",
    )(qk, qk, qk, qk, v, v, v)


MLSTM_CHUNKS = 8


def _mlstm_stage1a(q, k, li_row, lf_row, tri):
    return dict(qk=lax.dot_general(q, k, _NT, preferred_element_type=F32), li_row=li_row,
                cum_col=jnp.sum(jnp.where(tri, lf_row, 0.0), axis=1, keepdims=True),
                tot=jnp.sum(lf_row, axis=1, keepdims=True))


def _mlstm_stage1b(rec, tri, eye):
    cum_row = jnp.sum(jnp.where(eye, rec["cum_col"], 0.0), axis=0, keepdims=True)
    li_row = rec.pop("li_row")
    dmat = jnp.where(tri, rec["cum_col"] - cum_row + li_row, NEG)
    a = rec["tot"] - cum_row + li_row
    rec.update(dmat=dmat, a=a, m_loc=jnp.max(dmat, axis=1, keepdims=True), a_max=jnp.max(a, axis=1, keepdims=True))


def _mlstm_stage1c(rec):
    rec.update(p=jnp.exp(rec.pop("dmat") - rec["m_loc"]), w_k=jnp.exp(rec.pop("a") - rec["a_max"]))


def _mlstm_stage2(rec, kt):
    sc = rec["qk"] * rec["p"]
    w_k = rec["w_k"]
    rec.update(sc=sc.astype(BF16), den=jnp.sum(sc, axis=1, keepdims=True),
               kwt=(kt.astype(F32) * w_k).astype(BF16),
               w_rows=jnp.broadcast_to(w_k, (SUBLANES, w_k.shape[1])).astype(BF16))


def _mlstm_stage3(rec, k, v, vt):
    rec.update(num=jnp.dot(rec["sc"], v, preferred_element_type=F32),
               upd=lax.dot_general(rec["kwt"], vt, _NT, preferred_element_type=F32),
               n_upd=jnp.dot(rec["w_rows"], k, preferred_element_type=F32)[0:1, :])


def _mlstm_stage4a(rec, m_prev):
    m_inter = rec["cum_col"] + m_prev
    m_t = jnp.maximum(m_inter, rec["m_loc"])
    m_new = jnp.maximum(rec["tot"] + m_prev, rec["a_max"])
    rec.update(r_loc=jnp.exp(rec["m_loc"] - m_t), w_inter=jnp.exp(m_inter - m_t), floor=jnp.exp(-m_t),
               alpha=jnp.exp(rec["tot"] + m_prev - m_new), beta=jnp.exp(rec["a_max"] - m_new))
    return m_new


def _mlstm_stage4b(rec, q, state):
    c_prev, n_prev = state
    qc = jnp.dot(q, c_prev.astype(BF16), preferred_element_type=F32)
    qn = jnp.sum(q.astype(F32) * n_prev, axis=1, keepdims=True)
    r_loc, w_inter, alpha, beta = rec["r_loc"], rec["w_inter"], rec["alpha"], rec["beta"]
    den = r_loc * rec["den"] + w_inter * qn
    h = (r_loc * rec["num"] + w_inter * qc) * (1.0 / jnp.maximum(jnp.abs(den), rec["floor"]))
    return h, (alpha * c_prev + beta * rec["upd"], alpha * n_prev + beta * rec["n_upd"])


def _mlstm_kernel(qf, kf, vf, ktf, vtf, gf, qb, kb, vb, ktb, vtb, gb, hf_ref, hb_ref, c_scr, n_scr, m_scr):
    @pl.when(pl.program_id(2) == 0)
    def _():
        c_scr[...] = jnp.zeros_like(c_scr)
        n_scr[...] = jnp.zeros_like(n_scr)
        m_scr[...] = jnp.zeros_like(m_scr)

    L = M_CHUNK
    ti = lax.broadcasted_iota(jnp.int32, (L, L), 0)
    si = lax.broadcasted_iota(jnp.int32, (L, L), 1)
    eye = ti == si
    f_rows = [slice(c * L, (c + 1) * L) for c in range(MLSTM_CHUNKS)]
    dirs = (((qf, kf, vf, ktf, vtf, gf), (0, 1), si <= ti, f_rows, hf_ref),
            ((qb, kb, vb, ktb, vtb, gb), (2, 3), si >= ti, f_rows[::-1], hb_ref))
    recs = [[], []]
    for c in range(MLSTM_CHUNKS):
        for d, ((q, k, _, _, _, g), (gi, gf_), tri, rows, _) in enumerate(dirs):
            r = rows[c]
            recs[d].append(_mlstm_stage1a(q[r, :], k[r, :], g[gi:gi + 1, r], g[gf_:gf_ + 1, r], tri))
    for c in range(MLSTM_CHUNKS):
        for d in range(2):
            _mlstm_stage1b(recs[d][c], dirs[d][2], eye)
    for c in range(MLSTM_CHUNKS):
        for d in range(2):
            _mlstm_stage1c(recs[d][c])
    for c in range(MLSTM_CHUNKS):
        for d, ((_, _, _, kt, _, _), _, _, rows, _) in enumerate(dirs):
            _mlstm_stage2(recs[d][c], kt[:, rows[c]])
    for c in range(MLSTM_CHUNKS):
        for d, ((_, k, v, _, vt, _), _, _, rows, _) in enumerate(dirs):
            _mlstm_stage3(recs[d][c], k[rows[c], :], v[rows[c], :], vt[:, rows[c]])
    m = [m_scr[0], m_scr[1]]
    for c in range(MLSTM_CHUNKS):
        for d in range(2):
            m[d] = _mlstm_stage4a(recs[d][c], m[d])
    state = [(c_scr[0], n_scr[0]), (c_scr[1], n_scr[1])]
    for c in range(MLSTM_CHUNKS):
        for d, ((q, _, _, _, _, _), _, _, rows, h_ref) in enumerate(dirs):
            h, state[d] = _mlstm_stage4b(recs[d][c], q[rows[c], :], state[d])
            h_ref[rows[c], :] = h.astype(h_ref.dtype)
    for d in range(2):
        c_scr[d] = state[d][0]
        n_scr[d] = state[d][1]
        m_scr[d] = m[d]


def _mlstm(qk, v, kt, vt, gates, nb, s):
    t = v.shape[0]
    rows = MLSTM_CHUNKS * M_CHUNK
    nr = s // rows
    fblk = lambda b, j: b * nr + j
    bblk = lambda b, j: b * nr + (nr - 1 - j)

    def specs(blk):
        return [pl.BlockSpec((rows, M_HEAD_DIM), lambda b, h, j: (blk(b, j), h)),
                pl.BlockSpec((rows, M_HEAD_DIM), lambda b, h, j: (blk(b, j), M_HEADS + h)),
                pl.BlockSpec((rows, M_HEAD_DIM), lambda b, h, j: (blk(b, j), h)),
                pl.BlockSpec((M_HEAD_DIM, rows), lambda b, h, j: (h, blk(b, j))),
                pl.BlockSpec((M_HEAD_DIM, rows), lambda b, h, j: (h, blk(b, j))),
                pl.BlockSpec((SUBLANES, rows), lambda b, h, j: (h, blk(b, j)))]

    out = lambda blk: pl.BlockSpec((rows, M_HEAD_DIM), lambda b, h, j: (blk(b, j), h))
    return pl.pallas_call(
        _mlstm_kernel,
        grid=(nb, M_HEADS, nr),
        in_specs=specs(fblk) + specs(bblk),
        out_specs=(out(fblk), out(bblk)),
        out_shape=(jax.ShapeDtypeStruct((t, M_WIDTH), BF16), jax.ShapeDtypeStruct((t, M_WIDTH), BF16)),
        scratch_shapes=[pltpu.VMEM((2, M_HEAD_DIM, M_HEAD_DIM), F32), pltpu.VMEM((2, 1, M_HEAD_DIM), F32),
                        pltpu.VMEM((2, 1, 1), F32)],
        compiler_params=_params("parallel", "parallel", "arbitrary"),
        name="mlstm",
    )(qk, qk, v, kt, vt, gates, qk, qk, v, kt, vt, gates)


LSE_PAD = 128


def _expand_matrix():
    r = np.arange(LSE_PAD)[:, None]
    c = np.arange(N_GROUPS * ATTN_W)[None, :]
    hit = (r < N_GROUPS * 2 * HEADS_PER_GROUP) & (c // ATTN_W == r // (2 * HEADS_PER_GROUP)) \
        & ((c % ATTN_W) // HEAD_DIM == r % HEADS_PER_GROUP)
    return jnp.asarray(hit, dtype=BF16)


def _mix_out_kernel(o0, o1, o2, l0, l1, l2, hf, hb, mo, bg, x_ref, wa, wm, wo, gn, gpost, expand, y_ref,
                    wpad, os1, os2, *, tm):
    hs = hf[...].astype(F32) + hb[...].astype(F32)
    parts = []
    for h in range(M_HEADS):
        p = hs[:, h * M_HEAD_DIM:(h + 1) * M_HEAD_DIM]
        parts.append(p * lax.rsqrt(jnp.mean(p * p, axis=-1, keepdims=True) + EPS))
    hn = jnp.concatenate(parts, axis=-1) * gn[...]
    ml = jax.nn.sigmoid(mo[...].astype(F32)) * hn
    ml_proj = jnp.dot(ml.astype(BF16), wm[...], preferred_element_type=F32)
    lses = [l0[...], l1[...], l2[...]]
    mx = jnp.maximum(jnp.maximum(lses[0], lses[1]), lses[2])
    es = [jnp.exp(l - mx) for l in lses]
    inv_z = 1.0 / (es[0] + es[1] + es[2])
    wpad[...] = jnp.zeros_like(wpad)
    for g, e in enumerate(es):
        w = e * inv_z
        w_hi = w.astype(BF16).astype(F32)
        base = g * 2 * HEADS_PER_GROUP
        wpad[:, base:base + HEADS_PER_GROUP] = w_hi
        wpad[:, base + HEADS_PER_GROUP:base + 2 * HEADS_PER_GROUP] = w - w_hi
    wide = jnp.dot(wpad[...].astype(BF16), expand[...], preferred_element_type=F32)
    for o, osc, d in ((o1, os1, DILATIONS[1]), (o2, os2, DILATIONS[2])):
        n = tm // d
        for c in range(d):
            for j in range(ATTN_W // LANES):
                lo = c * ATTN_W + j * LANES
                osc[j, pl.ds(c, n, stride=d), :] = o[:, lo:lo + LANES].astype(F32)
    outs = [o0[...].astype(F32)] + [jnp.concatenate([osc[j] for j in range(ATTN_W // LANES)], axis=1)
                                    for osc in (os1, os2)]
    attn = (wide[:, :ATTN_W] * outs[0] + wide[:, ATTN_W:2 * ATTN_W] * outs[1] + wide[:, 2 * ATTN_W:] * outs[2])
    attn_proj = jnp.dot(attn.astype(BF16), wa[...], preferred_element_type=F32)
    gates = jax.nn.sigmoid(bg[...].astype(F32))
    merged = gates[:, :D_MODEL] * attn_proj + gates[:, D_MODEL:] * ml_proj
    mix = jnp.dot(merged.astype(BF16), wo[...], preferred_element_type=F32)
    y_ref[...] = x_ref[...] + _rms(mix, gpost[...])


def _mix_out(outs, lses, hf, hb, mo, bg, x, p, tm):
    t = x.shape[0]
    row = lambda w: pl.BlockSpec((tm, w), lambda i: (i, 0))
    view = lambda d: pl.BlockSpec((tm // d, d * ATTN_W), lambda i: (i, 0))
    return pl.pallas_call(
        functools.partial(_mix_out_kernel, tm=tm),
        grid=(t // tm,),
        in_specs=[row(ATTN_W), view(4), view(16),
                  row(HEADS_PER_GROUP), row(HEADS_PER_GROUP), row(HEADS_PER_GROUP),
                  row(M_WIDTH), row(M_WIDTH), row(M_WIDTH), row(2 * D_MODEL), row(D_MODEL),
                  _resident((ATTN_W, D_MODEL)), _resident((M_WIDTH, D_MODEL)), _resident((D_MODEL, D_MODEL)),
                  _resident((1, M_WIDTH)), _resident((1, D_MODEL)), _resident((LSE_PAD, N_GROUPS * ATTN_W))],
        out_specs=row(D_MODEL),
        out_shape=jax.ShapeDtypeStruct((t, D_MODEL), F32),
        scratch_shapes=[pltpu.VMEM((tm, LSE_PAD), F32),
                        pltpu.VMEM((ATTN_W // LANES, tm, LANES), F32),
                        pltpu.VMEM((ATTN_W // LANES, tm, LANES), F32)],
        compiler_params=_params("parallel"),
        name="mix_out",
    )(*outs, *lses, hf, hb, mo, bg, x, p["w_attn_proj"], p["w_mlstm_proj"], p["w_out"],
      p["g_mlstm_norm"], p["norm_mix_post"], _expand_matrix())


MXU_TILE = 256
FFN_SPLITS = (0, 6 * MXU_TILE, FFN_HIDDEN)
assert FFN_HIDDEN % MXU_TILE == 0


def _ffn_kernel(x_ref, gpre, win, wout, gpost, y_ref):
    x = x_ref[...]
    xb = _rms(x, gpre[...]).astype(BF16)
    acc = jnp.zeros(x.shape, F32)
    for lo, hi in zip(FFN_SPLITS[:-1], FFN_SPLITS[1:]):
        gate = jnp.dot(xb, win[:, lo:hi], preferred_element_type=F32)
        up = jnp.dot(xb, win[:, FFN_HIDDEN + lo:FFN_HIDDEN + hi], preferred_element_type=F32)
        act = (gate * jax.nn.sigmoid(gate) * up).astype(BF16)
        acc = acc + jnp.dot(act, wout[lo:hi, :], preferred_element_type=F32)
    y_ref[...] = x + _rms(acc, gpost[...])


def _ffn(x, p, tm):
    t = x.shape[0]
    row = pl.BlockSpec((tm, D_MODEL), lambda i: (i, 0))
    return pl.pallas_call(
        _ffn_kernel,
        grid=(t // tm,),
        in_specs=[row, _resident((1, D_MODEL)), _resident((D_MODEL, 2 * FFN_HIDDEN)),
                  _resident((FFN_HIDDEN, D_MODEL)), _resident((1, D_MODEL))],
        out_specs=row,
        out_shape=jax.ShapeDtypeStruct((t, D_MODEL), F32),
        compiler_params=_params("parallel"),
        name="ffn",
    )(x, p["norm_ffn_pre"], p["w_ffn_in"], p["w_ffn_out"], p["norm_ffn_post"])


TM_PROJ = 256
TM_MIX = 512
TM_FFN = 512


def _layer(x, nb, s, p):
    qk0, qk1, qk2, v0, v1, v2, mqk, mv, mo, bg, gates, mkt, mvt = _inproj(x, p, s, TM_PROJ)
    attn = [_attention_group(qk, v, g, nb, s) for g, (qk, v) in enumerate(((qk0, v0), (qk1, v1), (qk2, v2)))]
    hf, hb = _mlstm(mqk, mv, mkt, mvt, gates, nb, s)
    lses = [a[1].transpose(1, 0, 2).reshape(x.shape[0], HEADS_PER_GROUP) for a in attn]
    x = _mix_out([a[0] for a in attn], lses, hf, hb, mo, bg, x, p, TM_MIX)
    return _ffn(x, p, TM_FFN)


def _prepare(norm_mix_pre, norm_mix_post, norm_ffn_pre, norm_ffn_post, w_in, b_mlstm_gates, w_conv, b_conv,
             g_mlstm_norm, w_attn_proj, w_mlstm_proj, w_out, w_ffn_in, w_ffn_out):
    depth = w_in.shape[0]
    sizes = (3 * ATTN_W, 3 * ATTN_W, ATTN_W, 2 * M_WIDTH, M_WIDTH, M_WIDTH, 4 * M_HEADS, 2 * D_MODEL)
    offs = [int(c) for c in np.cumsum((0,) + sizes)]
    aq, ak, av, mqk, mv, mo, mg, bg = (w_in[:, :, offs[j]:offs[j + 1]] for j in range(8))
    aq = aq * (HEAD_DIM ** -0.5)
    qk = [jnp.concatenate([aq[:, :, g * ATTN_W:(g + 1) * ATTN_W], ak[:, :, g * ATTN_W:(g + 1) * ATTN_W]], axis=-1)
          for g in range(N_GROUPS)]
    w_main = jnp.concatenate(qk + [av, mqk, mv, mo, bg], axis=-1).astype(BF16)
    wg = mg.reshape(depth, D_MODEL, 4, M_HEADS).transpose(0, 3, 2, 1)
    wg = jnp.pad(wg, ((0, 0), (0, 0), (0, SUBLANES - 4), (0, 0))).reshape(depth, GATE_ROWS, D_MODEL).astype(BF16)
    bgate = b_mlstm_gates.reshape(depth, 4, M_HEADS).transpose(0, 2, 1)
    bgate = jnp.pad(bgate, ((0, 0), (0, 0), (0, SUBLANES - 4))).reshape(depth, GATE_ROWS, 1)
    layers = []
    for l in range(depth):
        layers.append(dict(
            norm_mix_pre=norm_mix_pre[l][None], norm_mix_post=norm_mix_post[l][None],
            norm_ffn_pre=norm_ffn_pre[l][None], norm_ffn_post=norm_ffn_post[l][None],
            w_main=w_main[l], w_gate=wg[l], b_gate=bgate[l],
            w_conv=w_conv[l], b_conv=b_conv[l][None], g_mlstm_norm=g_mlstm_norm[l][None],
            w_attn_proj=w_attn_proj[l].astype(BF16), w_mlstm_proj=w_mlstm_proj[l].astype(BF16),
            w_out=w_out[l].astype(BF16), w_ffn_in=w_ffn_in[l].astype(BF16), w_ffn_out=w_ffn_out[l].astype(BF16)))
    return layers


def _trunk(x, layers):
    nb, s, d = x.shape
    y = x.reshape(nb * s, d)
    for p in layers:
        y = _layer(y, nb, s, p)
    return y.reshape(nb, s, d)


def kernel(x_prompt, x_sample, norm_mix_pre, norm_mix_post, norm_ffn_pre, norm_ffn_post, w_in, b_mlstm_gates,
           w_conv, b_conv, g_mlstm_norm, w_attn_proj, w_mlstm_proj, w_out, w_ffn_in, w_ffn_out):
    layers = _prepare(norm_mix_pre, norm_mix_post, norm_ffn_pre, norm_ffn_post, w_in, b_mlstm_gates, w_conv,
                      b_conv, g_mlstm_norm, w_attn_proj, w_mlstm_proj, w_out, w_ffn_in, w_ffn_out)
    return _trunk(x_prompt, layers), _trunk(x_sample, layers)
```

```python
import functools

import numpy as np
import jax
import jax.numpy as jnp
from jax import lax
from jax.experimental import pallas as pl
from jax.experimental.pallas import tpu as pltpu

F32 = jnp.float32
BF16 = jnp.bfloat16

D_MODEL = 1024
ATTN_GROUPS = ((128, 1), (512, 4), (2048, 16))
DILATIONS = tuple(d for _, d in ATTN_GROUPS)
N_GROUPS = 3
HEADS_PER_GROUP = 8
HEAD_DIM = 64
HALF_WIN = 64
ATTN_W = HEADS_PER_GROUP * HEAD_DIM
M_WIDTH = 1024
M_HEADS = 4
M_HEAD_DIM = 256
M_CHUNK = 128
CONV_WIDTH = 5
FFN_HIDDEN = 2816
EPS = 1e-6
NEG = -1e30
SUBLANES = 8
GATE_ROWS = M_HEADS * SUBLANES

OFF_QK = 0
OFF_V = OFF_QK + 2 * N_GROUPS * ATTN_W
OFF_MQK = OFF_V + ATTN_W
OFF_MV = OFF_MQK + 2 * M_WIDTH
OFF_MO = OFF_MV + M_WIDTH
OFF_BG = OFF_MO + M_WIDTH
N_MAIN = OFF_BG + 2 * D_MODEL

VMEM_LIMIT = 56 * 1024 * 1024


def _alibi_slopes():
    h = np.arange(1, N_GROUPS * HEADS_PER_GROUP + 1, dtype=np.float32)
    return (2.0 ** (-8.0 * h / (N_GROUPS * HEADS_PER_GROUP))).astype(np.float32).reshape(N_GROUPS, HEADS_PER_GROUP)


def _rms(x, gain):
    return x * lax.rsqrt(jnp.mean(x * x, axis=-1, keepdims=True) + EPS) * gain


def _resident(shape):
    nd = len(shape)
    return pl.BlockSpec(shape, lambda *_: (0,) * nd, pipeline_mode=pl.Buffered(1))


_NT = (((1,), (1,)), ((), ()))
LANES = 128


def _slab_store(ref, row0, value):
    rows = value.shape[0]
    for j in range(ref.shape[0]):
        ref[j, row0:row0 + rows, :] = value[:, j * LANES:(j + 1) * LANES]


def _slab_rows(ref, start, size, stride):
    return jnp.concatenate([ref[j, pl.ds(start, size, stride=stride), :] for j in range(ref.shape[0])], axis=1)


def _params(*sem):
    return pltpu.CompilerParams(dimension_semantics=sem, vmem_limit_bytes=VMEM_LIMIT)


def _inproj_kernel(x_ref, xp_ref, xn_ref, gain_ref, w_ref, wg_ref, bias_ref, wc_ref, bc_ref,
                   qk0, qk1, qk2, v0, v1, v2, mqk, mv, mo, bg, gates, mkt, mvt, xs, vs, *, tm, tiles_per_seq):
    i = pl.program_id(0)
    gain = gain_ref[...]
    first = (i % tiles_per_seq) == 0
    last = (i % tiles_per_seq) == tiles_per_seq - 1
    xn = _rms(x_ref[...], gain)
    x_prev = jnp.where(first, 0.0, _rms(xp_ref[...], gain))
    x_next = jnp.where(last, 0.0, _rms(xn_ref[...], gain))
    _slab_store(xs, 0, xn)
    xb = xn.astype(BF16)
    xe = jnp.concatenate([x_prev, xn, x_next], axis=0).astype(BF16)
    ext = tm + 2 * SUBLANES
    cw = ATTN_W

    def proj(lhs, a, b):
        return jnp.dot(lhs, w_ref[:, a:b], preferred_element_type=F32)

    def conv_proj(cb):
        return proj(xe, OFF_MQK + cb * cw, OFF_MQK + (cb + 1) * cw)

    def conv_finish(cb, pm):
        acc = bc_ref[:, cb * cw:(cb + 1) * cw]
        for tap in range(CONV_WIDTH):
            shift = (CONV_WIDTH // 2 - tap) % ext
            shifted = pm if shift == 0 else pltpu.roll(pm, shift, 0)
            acc = acc + shifted[SUBLANES:SUBLANES + tm, :] * wc_ref[tap:tap + 1, cb * cw:(cb + 1) * cw]
        y = acc * jax.nn.sigmoid(acc)
        if cb < M_WIDTH // cw:
            y = y * (M_HEAD_DIM ** -0.5)
        mqk[:, cb * cw:(cb + 1) * cw] = y.astype(BF16)
        if cb >= M_WIDTH // cw:
            r0 = cb * cw - M_WIDTH
            mkt[r0:r0 + cw, :] = y.T.astype(BF16)

    def permuted_qk(g, ref):
        d = DILATIONS[g]
        n = tm // d
        xperm = jnp.concatenate([_slab_rows(xs, c, n, d) for c in range(d)], axis=0)
        pg = proj(xperm.astype(BF16), OFF_QK + g * 2 * ATTN_W, OFF_QK + (g + 1) * 2 * ATTN_W).astype(BF16)
        for c in range(d):
            ref[:, c * 2 * ATTN_W:(c + 1) * 2 * ATTN_W] = pg[c * n:(c + 1) * n, :]

    pm = conv_proj(0)
    qk0[...] = proj(xb, OFF_QK, OFF_QK + 2 * ATTN_W).astype(BF16)
    conv_finish(0, pm)
    pm = conv_proj(1)
    permuted_qk(1, qk1)
    conv_finish(1, pm)
    pm = conv_proj(2)
    permuted_qk(2, qk2)
    conv_finish(2, pm)
    pm = conv_proj(3)
    vf = proj(xb, OFF_V, OFF_V + ATTN_W)
    v0[...] = vf.astype(BF16)
    _slab_store(vs, 0, vf)
    for g, ref in ((1, v1), (2, v2)):
        d = DILATIONS[g]
        for c in range(d):
            ref[:, c * ATTN_W:(c + 1) * ATTN_W] = _slab_rows(vs, c, tm // d, d).astype(BF16)
    vm = proj(xb, OFF_MV, OFF_MV + M_WIDTH)
    mv[...] = vm.astype(BF16)
    mvt[...] = vm.T.astype(BF16)
    conv_finish(3, pm)
    mo[...] = proj(xb, OFF_MO, OFF_MO + M_WIDTH).astype(BF16)
    bg[...] = proj(xb, OFF_BG, OFF_BG + 2 * D_MODEL).astype(BF16)
    gp = lax.dot_general(wg_ref[...], xb, _NT, preferred_element_type=F32) + bias_ref[...]
    row = lax.broadcasted_iota(jnp.int32, gp.shape, 0)
    log_sig = jnp.minimum(gp, 0.0) - jnp.log1p(jnp.exp(-jnp.abs(gp)))
    gates[...] = jnp.where(row % 2 == 1, log_sig, gp)


def _inproj(x, p, s, tm):
    t = x.shape[0]
    hb = tm // SUBLANES
    n_halo = t // SUBLANES
    row = lambda w: pl.BlockSpec((tm, w), lambda i: (i, 0))
    view = lambda d, w: pl.BlockSpec((tm // d, d * w), lambda i: (i, 0))
    out_specs = (row(2 * ATTN_W), view(4, 2 * ATTN_W), view(16, 2 * ATTN_W),
                 row(ATTN_W), view(4, ATTN_W), view(16, ATTN_W),
                 row(2 * M_WIDTH), row(M_WIDTH), row(M_WIDTH), row(2 * D_MODEL),
                 pl.BlockSpec((GATE_ROWS, tm), lambda i: (0, i)),
                 pl.BlockSpec((M_WIDTH, tm), lambda i: (0, i)), pl.BlockSpec((M_WIDTH, tm), lambda i: (0, i)))
    bshape = lambda d, w: jax.ShapeDtypeStruct((t // d, d * w), BF16)
    out_shape = (bshape(1, 2 * ATTN_W), bshape(4, 2 * ATTN_W), bshape(16, 2 * ATTN_W),
                 bshape(1, ATTN_W), bshape(4, ATTN_W), bshape(16, ATTN_W),
                 bshape(1, 2 * M_WIDTH), bshape(1, M_WIDTH), bshape(1, M_WIDTH), bshape(1, 2 * D_MODEL),
                 jax.ShapeDtypeStruct((GATE_ROWS, t), F32),
                 jax.ShapeDtypeStruct((M_WIDTH, t), BF16), jax.ShapeDtypeStruct((M_WIDTH, t), BF16))
    return pl.pallas_call(
        functools.partial(_inproj_kernel, tm=tm, tiles_per_seq=s // tm),
        grid=(t // tm,),
        in_specs=[row(D_MODEL),
                  pl.BlockSpec((SUBLANES, D_MODEL), lambda i: (jnp.maximum(i * hb - 1, 0), 0)),
                  pl.BlockSpec((SUBLANES, D_MODEL), lambda i: (jnp.minimum((i + 1) * hb, n_halo - 1), 0)),
                  _resident((1, D_MODEL)), _resident((D_MODEL, N_MAIN)), _resident((GATE_ROWS, D_MODEL)),
                  _resident((GATE_ROWS, 1)), _resident((CONV_WIDTH, 2 * M_WIDTH)), _resident((1, 2 * M_WIDTH))],
        out_specs=out_specs,
        out_shape=out_shape,
        scratch_shapes=[pltpu.VMEM((D_MODEL // LANES, tm, LANES), F32), pltpu.VMEM((ATTN_W // LANES, tm, LANES), F32)],
        compiler_params=_params("parallel"),
        name="inproj",
    )(x, x, x, p["norm_mix_pre"], p["w_main"], p["w_gate"], p["b_gate"], p["w_conv"], p["b_conv"])


ATTN_TILE = 128
ATTN_KEYS = ATTN_TILE + 2 * HALF_WIN
ATTN_BLOCK = 1024
PAIR_W = 2 * HEAD_DIM


def _attn_init(bias, vw, *, slopes, dilation):
    qi = lax.broadcasted_iota(jnp.int32, (ATTN_TILE, ATTN_KEYS), 0)
    kj = lax.broadcasted_iota(jnp.int32, (ATTN_TILE, ATTN_KEYS), 1)
    step = jnp.abs(kj - HALF_WIN - qi)
    dist = (step * dilation).astype(F32)
    band = jnp.where(step <= HALF_WIN, 0.0, NEG)
    for variant in range(4):
        mask = band
        if variant & 1:
            mask = jnp.where(kj < HALF_WIN, NEG, mask)
        if variant & 2:
            mask = jnp.where(kj >= ATTN_TILE + HALF_WIN, NEG, mask)
        for h in range(HEADS_PER_GROUP):
            bias[variant, h] = -float(slopes[h]) * dist + mask
    ones = jnp.ones((vw.shape[0], PAIR_W), BF16)
    for pair in range(HEADS_PER_GROUP // 2):
        vw[:, (2 * pair + 1) * PAIR_W:(2 * pair + 2) * PAIR_W] = ones


def _attn_kernel(q_ref, k_ref, kp_ref, kn_ref, v_ref, vp_ref, vn_ref, o_ref, lse_ref, kw, vw, bias,
                 *, slopes, dilation, tqb, nq):
    i = pl.program_id(2)

    @pl.when((pl.program_id(0) == 0) & (pl.program_id(1) == 0) & (i == 0))
    def _():
        _attn_init(bias, vw, slopes=slopes, dilation=dilation)

    kw[0:HALF_WIN, :] = kp_ref[...]
    kw[HALF_WIN:HALF_WIN + tqb, :] = k_ref[...]
    kw[HALF_WIN + tqb:, :] = kn_ref[...]
    for pair in range(HEADS_PER_GROUP // 2):
        src = slice(pair * PAIR_W, (pair + 1) * PAIR_W)
        dst = slice(2 * pair * PAIR_W, (2 * pair + 1) * PAIR_W)
        vw[0:HALF_WIN, dst] = vp_ref[:, src]
        vw[HALF_WIN:HALF_WIN + tqb, dst] = v_ref[:, src]
        vw[HALF_WIN + tqb:, dst] = vn_ref[:, src]

    lane = lax.broadcasted_iota(jnp.int32, (ATTN_TILE, PAIR_W), 1)
    low = lane < HEAD_DIM
    n_tiles = tqb // ATTN_TILE
    for t in range(n_tiles):
        variant = 0
        if t == 0:
            variant = variant + jnp.where(i == 0, 1, 0)
        if t == n_tiles - 1:
            variant = variant + jnp.where(i == nq - 1, 2, 0)
        rows = slice(t * ATTN_TILE, (t + 1) * ATTN_TILE)
        krows = slice(t * ATTN_TILE, t * ATTN_TILE + ATTN_KEYS)
        heads = range(HEADS_PER_GROUP)
        pcols = [slice((h // 2) * PAIR_W, (h // 2 + 1) * PAIR_W) for h in heads]
        scores = []
        for h in heads:
            q2 = q_ref[rows, pcols[h]]
            qh = jnp.where(low if h % 2 == 0 else ~low, q2, jnp.zeros_like(q2))
            scores.append(lax.dot_general(qh, kw[krows, pcols[h]], _NT, preferred_element_type=F32))
        probs, maxes = [], []
        for h in heads:
            s = scores[h] + bias[variant, h]
            m = jnp.max(s, axis=-1, keepdims=True)
            probs.append(jnp.exp(s - m).astype(BF16))
            maxes.append(m)
        pvs = [jnp.dot(probs[h], vw[krows, (h // 2) * 2 * PAIR_W:(h // 2 + 1) * 2 * PAIR_W],
                       preferred_element_type=F32) for h in heads]
        lse = jnp.zeros((ATTN_TILE, PAIR_W), F32)
        outs = []
        for h in heads:
            l = pvs[h][:, PAIR_W:]
            outs.append(pvs[h][:, :PAIR_W] * (1.0 / l))
            lse = jnp.where(lane == h, maxes[h] + jnp.log(l), lse)
        lse_ref[rows, :] = lse[:, 0:HEADS_PER_GROUP]
        for pair in range(HEADS_PER_GROUP // 2):
            o_ref[rows, pcols[2 * pair]] = jnp.where(low, outs[2 * pair], outs[2 * pair + 1]).astype(o_ref.dtype)


def _attention_group(qk, v, g, nb, s):
    dilation = DILATIONS[g]
    rows = qk.shape[0]
    sub_len = s // dilation
    tqb = min(ATTN_BLOCK, sub_len)
    nq = sub_len // tqb
    hb = tqb // HALF_WIN
    n_halo = rows // HALF_WIN

    def spec(size, row_fn, col_fn):
        return pl.BlockSpec((size, ATTN_W), lambda b, c, i: (row_fn(b * nq + i), col_fn(c)))

    cur = lambda r: r
    prv = lambda r: jnp.maximum(r * hb - 1, 0)
    nxt = lambda r: jnp.minimum((r + 1) * hb, n_halo - 1)
    qcol, kcol, vcol = (lambda c: 2 * c), (lambda c: 2 * c + 1), (lambda c: c)
    kern = functools.partial(_attn_kernel, slopes=tuple(_alibi_slopes()[g]), dilation=dilation, tqb=tqb, nq=nq)
    return pl.pallas_call(
        kern,
        grid=(nb, dilation, nq),
        in_specs=[spec(tqb, cur, qcol), spec(tqb, cur, kcol), spec(HALF_WIN, prv, kcol), spec(HALF_WIN, nxt, kcol),
                  spec(tqb, cur, vcol), spec(HALF_WIN, prv, vcol), spec(HALF_WIN, nxt, vcol)],
        out_specs=(spec(tqb, cur, vcol),
                   pl.BlockSpec((None, tqb, HEADS_PER_GROUP), lambda b, c, i: (c, b * nq + i, 0))),
        out_shape=(jax.ShapeDtypeStruct((rows, dilation * ATTN_W), BF16),
                   jax.ShapeDtypeStruct((dilation, rows, HEADS_PER_GROUP), F32)),
        scratch_shapes=[pltpu.VMEM((tqb + 2 * HALF_WIN, ATTN_W), BF16),
                        pltpu.VMEM((tqb + 2 * HALF_WIN, 2 * ATTN_W), BF16),
                        pltpu.VMEM((4, HEADS_PER_GROUP, ATTN_TILE, ATTN_KEYS), F32)],
        compiler_params=_params("arbitrary", "arbitrary", "arbitrary"),
        name=f"attn_g{g}",
    )(qk, qk, qk, qk, v, v, v)


MLSTM_CHUNKS = 8


def _mlstm_stage1a(q, k, li_row, lf_row, tri):
    return dict(qk=lax.dot_general(q, k, _NT, preferred_element_type=F32), li_row=li_row,
                cum_col=jnp.sum(jnp.where(tri, lf_row, 0.0), axis=1, keepdims=True),
                tot=jnp.sum(lf_row, axis=1, keepdims=True))


def _mlstm_stage1b(rec, tri, eye):
    cum_row = jnp.sum(jnp.where(eye, rec["cum_col"], 0.0), axis=0, keepdims=True)
    li_row = rec.pop("li_row")
    dmat = jnp.where(tri, rec["cum_col"] - cum_row + li_row, NEG)
    a = rec["tot"] - cum_row + li_row
    rec.update(dmat=dmat, a=a, m_loc=jnp.max(dmat, axis=1, keepdims=True), a_max=jnp.max(a, axis=1, keepdims=True))


def _mlstm_stage1c(rec, m_prev):
    m_inter = rec["cum_col"] + m_prev
    m_t = jnp.maximum(m_inter, rec.pop("m_loc"))
    m_new = jnp.maximum(rec["tot"] + m_prev, rec.pop("a_max"))
    rec.update(p=jnp.exp(rec.pop("dmat") - m_t), w_k=jnp.exp(rec.pop("a") - m_new),
               w_inter=jnp.exp(m_inter - m_t), floor=jnp.exp(-m_t), decay=jnp.exp(rec["tot"] + m_prev - m_new))
    return m_new


def _mlstm_stage2(rec, kt):
    sc = rec["qk"] * rec["p"]
    w_k = rec["w_k"]
    rec.update(sc=sc.astype(BF16), den=jnp.sum(sc, axis=1, keepdims=True),
               kwt=(kt.astype(F32) * w_k).astype(BF16),
               w_rows=jnp.broadcast_to(w_k, (SUBLANES, w_k.shape[1])).astype(BF16))


def _mlstm_stage3(rec, k, v, vt):
    rec.update(num=jnp.dot(rec["sc"], v, preferred_element_type=F32),
               upd=lax.dot_general(rec["kwt"], vt, _NT, preferred_element_type=F32),
               n_upd=jnp.dot(rec["w_rows"], k, preferred_element_type=F32)[0:1, :])


def _mlstm_stage4a(rec, state):
    c_prev, n_prev = state
    rec.update(c_in=c_prev.astype(BF16), n_in=n_prev)
    decay = rec.pop("decay")
    return decay * c_prev + rec.pop("upd"), decay * n_prev + rec.pop("n_upd")


def _mlstm_stage4b(rec, q):
    rec.update(qc=jnp.dot(q, rec.pop("c_in"), preferred_element_type=F32),
               qn=jnp.sum(q.astype(F32) * rec.pop("n_in"), axis=1, keepdims=True))


def _mlstm_stage4c(rec):
    w_inter = rec["w_inter"]
    den = rec["den"] + w_inter * rec["qn"]
    return (rec["num"] + w_inter * rec["qc"]) * (1.0 / jnp.maximum(jnp.abs(den), rec["floor"]))


def _mlstm_kernel(qf, kf, vf, ktf, vtf, gf, qb, kb, vb, ktb, vtb, gb, hf_ref, hb_ref, c_scr, n_scr, m_scr):
    @pl.when(pl.program_id(2) == 0)
    def _():
        c_scr[...] = jnp.zeros_like(c_scr)
        n_scr[...] = jnp.zeros_like(n_scr)
        m_scr[...] = jnp.zeros_like(m_scr)

    L = M_CHUNK
    ti = lax.broadcasted_iota(jnp.int32, (L, L), 0)
    si = lax.broadcasted_iota(jnp.int32, (L, L), 1)
    eye = ti == si
    f_rows = [slice(c * L, (c + 1) * L) for c in range(MLSTM_CHUNKS)]
    dirs = (((qf, kf, vf, ktf, vtf, gf), (0, 1), si <= ti, f_rows, hf_ref),
            ((qb, kb, vb, ktb, vtb, gb), (2, 3), si >= ti, f_rows[::-1], hb_ref))
    recs = [[], []]
    for c in range(MLSTM_CHUNKS):
        for d, ((q, k, _, _, _, g), (gi, gf_), tri, rows, _) in enumerate(dirs):
            r = rows[c]
            recs[d].append(_mlstm_stage1a(q[r, :], k[r, :], g[gi:gi + 1, r], g[gf_:gf_ + 1, r], tri))
    for c in range(MLSTM_CHUNKS):
        for d in range(2):
            _mlstm_stage1b(recs[d][c], dirs[d][2], eye)
    m = [m_scr[0], m_scr[1]]
    for c in range(MLSTM_CHUNKS):
        for d in range(2):
            m[d] = _mlstm_stage1c(recs[d][c], m[d])
    for c in range(MLSTM_CHUNKS):
        for d, ((_, _, _, kt, _, _), _, _, rows, _) in enumerate(dirs):
            _mlstm_stage2(recs[d][c], kt[:, rows[c]])
    for c in range(MLSTM_CHUNKS):
        for d, ((_, k, v, _, vt, _), _, _, rows, _) in enumerate(dirs):
            _mlstm_stage3(recs[d][c], k[rows[c], :], v[rows[c], :], vt[:, rows[c]])
    state = [(c_scr[0], n_scr[0]), (c_scr[1], n_scr[1])]
    for c in range(MLSTM_CHUNKS):
        for d in range(2):
            state[d] = _mlstm_stage4a(recs[d][c], state[d])
    for c in range(MLSTM_CHUNKS):
        for d, ((q, _, _, _, _, _), _, _, rows, _) in enumerate(dirs):
            _mlstm_stage4b(recs[d][c], q[rows[c], :])
    for c in range(MLSTM_CHUNKS):
        for d, (_, _, _, rows, h_ref) in enumerate(dirs):
            h_ref[rows[c], :] = _mlstm_stage4c(recs[d][c]).astype(h_ref.dtype)
    for d in range(2):
        c_scr[d] = state[d][0]
        n_scr[d] = state[d][1]
        m_scr[d] = m[d]


def _mlstm(qk, v, kt, vt, gates, nb, s):
    t = v.shape[0]
    rows = MLSTM_CHUNKS * M_CHUNK
    nr = s // rows
    fblk = lambda b, j: b * nr + j
    bblk = lambda b, j: b * nr + (nr - 1 - j)

    def specs(blk):
        return [pl.BlockSpec((rows, M_HEAD_DIM), lambda b, h, j: (blk(b, j), h)),
                pl.BlockSpec((rows, M_HEAD_DIM), lambda b, h, j: (blk(b, j), M_HEADS + h)),
                pl.BlockSpec((rows, M_HEAD_DIM), lambda b, h, j: (blk(b, j), h)),
                pl.BlockSpec((M_HEAD_DIM, rows), lambda b, h, j: (h, blk(b, j))),
                pl.BlockSpec((M_HEAD_DIM, rows), lambda b, h, j: (h, blk(b, j))),
                pl.BlockSpec((SUBLANES, rows), lambda b, h, j: (h, blk(b, j)))]

    out = lambda blk: pl.BlockSpec((rows, M_HEAD_DIM), lambda b, h, j: (blk(b, j), h))
    return pl.pallas_call(
        _mlstm_kernel,
        grid=(nb, M_HEADS, nr),
        in_specs=specs(fblk) + specs(bblk),
        out_specs=(out(fblk), out(bblk)),
        out_shape=(jax.ShapeDtypeStruct((t, M_WIDTH), BF16), jax.ShapeDtypeStruct((t, M_WIDTH), BF16)),
        scratch_shapes=[pltpu.VMEM((2, M_HEAD_DIM, M_HEAD_DIM), F32), pltpu.VMEM((2, 1, M_HEAD_DIM), F32),
                        pltpu.VMEM((2, 1, 1), F32)],
        compiler_params=_params("parallel", "parallel", "arbitrary"),
        name="mlstm",
    )(qk, qk, v, kt, vt, gates, qk, qk, v, kt, vt, gates)


LSE_PAD = 128


def _expand_matrix():
    r = np.arange(LSE_PAD)[:, None]
    c = np.arange(N_GROUPS * ATTN_W)[None, :]
    hit = (r < N_GROUPS * 2 * HEADS_PER_GROUP) & (c // ATTN_W == r // (2 * HEADS_PER_GROUP)) \
        & ((c % ATTN_W) // HEAD_DIM == r % HEADS_PER_GROUP)
    return jnp.asarray(hit, dtype=BF16)


def _mix_out_kernel(o0, o1, o2, l0, l1, l2, hf, hb, mo, bg, x_ref, wa, wm, wo, gn, gpost, expand, y_ref,
                    wpad, os1, os2, *, tm):
    hs = hf[...].astype(F32) + hb[...].astype(F32)
    parts = []
    for h in range(M_HEADS):
        p = hs[:, h * M_HEAD_DIM:(h + 1) * M_HEAD_DIM]
        parts.append(p * lax.rsqrt(jnp.mean(p * p, axis=-1, keepdims=True) + EPS))
    hn = jnp.concatenate(parts, axis=-1) * gn[...]
    ml = jax.nn.sigmoid(mo[...].astype(F32)) * hn
    ml_proj = jnp.dot(ml.astype(BF16), wm[...], preferred_element_type=F32)
    lses = [l0[...], l1[...], l2[...]]
    mx = jnp.maximum(jnp.maximum(lses[0], lses[1]), lses[2])
    es = [jnp.exp(l - mx) for l in lses]
    inv_z = 1.0 / (es[0] + es[1] + es[2])
    wpad[...] = jnp.zeros_like(wpad)
    for g, e in enumerate(es):
        w = e * inv_z
        w_hi = w.astype(BF16).astype(F32)
        base = g * 2 * HEADS_PER_GROUP
        wpad[:, base:base + HEADS_PER_GROUP] = w_hi
        wpad[:, base + HEADS_PER_GROUP:base + 2 * HEADS_PER_GROUP] = w - w_hi
    wide = jnp.dot(wpad[...].astype(BF16), expand[...], preferred_element_type=F32)
    for o, osc, d in ((o1, os1, DILATIONS[1]), (o2, os2, DILATIONS[2])):
        n = tm // d
        for c in range(d):
            for j in range(ATTN_W // LANES):
                lo = c * ATTN_W + j * LANES
                osc[j, pl.ds(c, n, stride=d), :] = o[:, lo:lo + LANES].astype(F32)
    outs = [o0[...].astype(F32)] + [jnp.concatenate([osc[j] for j in range(ATTN_W // LANES)], axis=1)
                                    for osc in (os1, os2)]
    attn = (wide[:, :ATTN_W] * outs[0] + wide[:, ATTN_W:2 * ATTN_W] * outs[1] + wide[:, 2 * ATTN_W:] * outs[2])
    attn_proj = jnp.dot(attn.astype(BF16), wa[...], preferred_element_type=F32)
    gates = jax.nn.sigmoid(bg[...].astype(F32))
    merged = gates[:, :D_MODEL] * attn_proj + gates[:, D_MODEL:] * ml_proj
    mix = jnp.dot(merged.astype(BF16), wo[...], preferred_element_type=F32)
    y_ref[...] = x_ref[...] + _rms(mix, gpost[...])


def _mix_out(outs, lses, hf, hb, mo, bg, x, p, tm):
    t = x.shape[0]
    row = lambda w: pl.BlockSpec((tm, w), lambda i: (i, 0))
    view = lambda d: pl.BlockSpec((tm // d, d * ATTN_W), lambda i: (i, 0))
    return pl.pallas_call(
        functools.partial(_mix_out_kernel, tm=tm),
        grid=(t // tm,),
        in_specs=[row(ATTN_W), view(4), view(16),
                  row(HEADS_PER_GROUP), row(HEADS_PER_GROUP), row(HEADS_PER_GROUP),
                  row(M_WIDTH), row(M_WIDTH), row(M_WIDTH), row(2 * D_MODEL), row(D_MODEL),
                  _resident((ATTN_W, D_MODEL)), _resident((M_WIDTH, D_MODEL)), _resident((D_MODEL, D_MODEL)),
                  _resident((1, M_WIDTH)), _resident((1, D_MODEL)), _resident((LSE_PAD, N_GROUPS * ATTN_W))],
        out_specs=row(D_MODEL),
        out_shape=jax.ShapeDtypeStruct((t, D_MODEL), F32),
        scratch_shapes=[pltpu.VMEM((tm, LSE_PAD), F32),
                        pltpu.VMEM((ATTN_W // LANES, tm, LANES), F32),
                        pltpu.VMEM((ATTN_W // LANES, tm, LANES), F32)],
        compiler_params=_params("parallel"),
        name="mix_out",
    )(*outs, *lses, hf, hb, mo, bg, x, p["w_attn_proj"], p["w_mlstm_proj"], p["w_out"],
      p["g_mlstm_norm"], p["norm_mix_post"], _expand_matrix())


MXU_TILE = 256
FFN_SPLITS = (0, 6 * MXU_TILE, FFN_HIDDEN)
assert FFN_HIDDEN % MXU_TILE == 0


def _ffn_kernel(x_ref, gpre, win, wout, gpost, y_ref):
    x = x_ref[...]
    xb = _rms(x, gpre[...]).astype(BF16)
    acc = jnp.zeros(x.shape, F32)
    for lo, hi in zip(FFN_SPLITS[:-1], FFN_SPLITS[1:]):
        gate = jnp.dot(xb, win[:, lo:hi], preferred_element_type=F32)
        up = jnp.dot(xb, win[:, FFN_HIDDEN + lo:FFN_HIDDEN + hi], preferred_element_type=F32)
        act = (gate * jax.nn.sigmoid(gate) * up).astype(BF16)
        acc = acc + jnp.dot(act, wout[lo:hi, :], preferred_element_type=F32)
    y_ref[...] = x + _rms(acc, gpost[...])


def _ffn(x, p, tm):
    t = x.shape[0]
    row = pl.BlockSpec((tm, D_MODEL), lambda i: (i, 0))
    return pl.pallas_call(
        _ffn_kernel,
        grid=(t // tm,),
        in_specs=[row, _resident((1, D_MODEL)), _resident((D_MODEL, 2 * FFN_HIDDEN)),
                  _resident((FFN_HIDDEN, D_MODEL)), _resident((1, D_MODEL))],
        out_specs=row,
        out_shape=jax.ShapeDtypeStruct((t, D_MODEL), F32),
        compiler_params=_params("parallel"),
        name="ffn",
    )(x, p["norm_ffn_pre"], p["w_ffn_in"], p["w_ffn_out"], p["norm_ffn_post"])


TM_PROJ = 256
TM_MIX = 512
TM_FFN = 512


def _layer(x, nb, s, p):
    qk0, qk1, qk2, v0, v1, v2, mqk, mv, mo, bg, gates, mkt, mvt = _inproj(x, p, s, TM_PROJ)
    attn = [_attention_group(qk, v, g, nb, s) for g, (qk, v) in enumerate(((qk0, v0), (qk1, v1), (qk2, v2)))]
    hf, hb = _mlstm(mqk, mv, mkt, mvt, gates, nb, s)
    lses = [a[1].transpose(1, 0, 2).reshape(x.shape[0], HEADS_PER_GROUP) for a in attn]
    x = _mix_out([a[0] for a in attn], lses, hf, hb, mo, bg, x, p, TM_MIX)
    return _ffn(x, p, TM_FFN)


def _prepare(norm_mix_pre, norm_mix_post, norm_ffn_pre, norm_ffn_post, w_in, b_mlstm_gates, w_conv, b_conv,
             g_mlstm_norm, w_attn_proj, w_mlstm_proj, w_out, w_ffn_in, w_ffn_out):
    depth = w_in.shape[0]
    sizes = (3 * ATTN_W, 3 * ATTN_W, ATTN_W, 2 * M_WIDTH, M_WIDTH, M_WIDTH, 4 * M_HEADS, 2 * D_MODEL)
    offs = [int(c) for c in np.cumsum((0,) + sizes)]
    aq, ak, av, mqk, mv, mo, mg, bg = (w_in[:, :, offs[j]:offs[j + 1]] for j in range(8))
    aq = aq * (HEAD_DIM ** -0.5)
    qk = [jnp.concatenate([aq[:, :, g * ATTN_W:(g + 1) * ATTN_W], ak[:, :, g * ATTN_W:(g + 1) * ATTN_W]], axis=-1)
          for g in range(N_GROUPS)]
    w_main = jnp.concatenate(qk + [av, mqk, mv, mo, bg], axis=-1).astype(BF16)
    wg = mg.reshape(depth, D_MODEL, 4, M_HEADS).transpose(0, 3, 2, 1)
    wg = jnp.pad(wg, ((0, 0), (0, 0), (0, SUBLANES - 4), (0, 0))).reshape(depth, GATE_ROWS, D_MODEL).astype(BF16)
    bgate = b_mlstm_gates.reshape(depth, 4, M_HEADS).transpose(0, 2, 1)
    bgate = jnp.pad(bgate, ((0, 0), (0, 0), (0, SUBLANES - 4))).reshape(depth, GATE_ROWS, 1)
    layers = []
    for l in range(depth):
        layers.append(dict(
            norm_mix_pre=norm_mix_pre[l][None], norm_mix_post=norm_mix_post[l][None],
            norm_ffn_pre=norm_ffn_pre[l][None], norm_ffn_post=norm_ffn_post[l][None],
            w_main=w_main[l], w_gate=wg[l], b_gate=bgate[l],
            w_conv=w_conv[l], b_conv=b_conv[l][None], g_mlstm_norm=g_mlstm_norm[l][None],
            w_attn_proj=w_attn_proj[l].astype(BF16), w_mlstm_proj=w_mlstm_proj[l].astype(BF16),
            w_out=w_out[l].astype(BF16), w_ffn_in=w_ffn_in[l].astype(BF16), w_ffn_out=w_ffn_out[l].astype(BF16)))
    return layers


def _trunk(x, layers):
    nb, s, d = x.shape
    y = x.reshape(nb * s, d)
    for p in layers:
        y = _layer(y, nb, s, p)
    return y.reshape(nb, s, d)


def kernel(x_prompt, x_sample, norm_mix_pre, norm_mix_post, norm_ffn_pre, norm_ffn_post, w_in, b_mlstm_gates,
           w_conv, b_conv, g_mlstm_norm, w_attn_proj, w_mlstm_proj, w_out, w_ffn_in, w_ffn_out):
    layers = _prepare(norm_mix_pre, norm_mix_post, norm_ffn_pre, norm_ffn_post, w_in, b_mlstm_gates, w_conv,
                      b_conv, g_mlstm_norm, w_attn_proj, w_mlstm_proj, w_out, w_ffn_in, w_ffn_out)
    return _trunk(x_prompt, layers), _trunk(x_sample, layers)
```

```python
import functools

import numpy as np
import jax
import jax.numpy as jnp
from jax import lax
from jax.experimental import pallas as pl
from jax.experimental.pallas import tpu as pltpu

F32 = jnp.float32
BF16 = jnp.bfloat16

D_MODEL = 1024
ATTN_GROUPS = ((128, 1), (512, 4), (2048, 16))
DILATIONS = tuple(d for _, d in ATTN_GROUPS)
N_GROUPS = 3
HEADS_PER_GROUP = 8
HEAD_DIM = 64
HALF_WIN = 64
ATTN_W = HEADS_PER_GROUP * HEAD_DIM
M_WIDTH = 1024
M_HEADS = 4
M_HEAD_DIM = 256
M_CHUNK = 128
CONV_WIDTH = 5
FFN_HIDDEN = 2816
EPS = 1e-6
NEG = -1e30
SUBLANES = 8
GATE_ROWS = M_HEADS * SUBLANES

OFF_QK = 0
OFF_V = OFF_QK + 2 * N_GROUPS * ATTN_W
OFF_MQK = OFF_V + ATTN_W
OFF_MV = OFF_MQK + 2 * M_WIDTH
OFF_MO = OFF_MV + M_WIDTH
OFF_BG = OFF_MO + M_WIDTH
N_MAIN = OFF_BG + 2 * D_MODEL

VMEM_LIMIT = 56 * 1024 * 1024


def _alibi_slopes():
    h = np.arange(1, N_GROUPS * HEADS_PER_GROUP + 1, dtype=np.float32)
    return (2.0 ** (-8.0 * h / (N_GROUPS * HEADS_PER_GROUP))).astype(np.float32).reshape(N_GROUPS, HEADS_PER_GROUP)


def _rms(x, gain):
    return x * lax.rsqrt(jnp.mean(x * x, axis=-1, keepdims=True) + EPS) * gain


def _resident(shape):
    nd = len(shape)
    return pl.BlockSpec(shape, lambda *_: (0,) * nd, pipeline_mode=pl.Buffered(1))


_NT = (((1,), (1,)), ((), ()))
LANES = 128


def _slab_store(ref, row0, value):
    rows = value.shape[0]
    for j in range(ref.shape[0]):
        ref[j, row0:row0 + rows, :] = value[:, j * LANES:(j + 1) * LANES]


def _slab_rows(ref, start, size, stride):
    return jnp.concatenate([ref[j, pl.ds(start, size, stride=stride), :] for j in range(ref.shape[0])], axis=1)


def _params(*sem):
    return pltpu.CompilerParams(dimension_semantics=sem, vmem_limit_bytes=VMEM_LIMIT)


def _inproj_kernel(x_ref, xp_ref, xn_ref, gain_ref, w_ref, wg_ref, bias_ref, wc_ref, bc_ref,
                   qk0, qk1, qk2, v0, v1, v2, mqk, mv, mo, bg, gates, mkt, mvt, xs, vs, *, tm, tiles_per_seq):
    i = pl.program_id(0)
    gain = gain_ref[...]
    first = (i % tiles_per_seq) == 0
    last = (i % tiles_per_seq) == tiles_per_seq - 1
    xn = _rms(x_ref[...], gain)
    x_prev = jnp.where(first, 0.0, _rms(xp_ref[...], gain))
    x_next = jnp.where(last, 0.0, _rms(xn_ref[...], gain))
    _slab_store(xs, 0, xn)
    xb = xn.astype(BF16)
    xe = jnp.concatenate([x_prev, xn, x_next], axis=0).astype(BF16)
    ext = tm + 2 * SUBLANES
    cw = ATTN_W

    def proj(lhs, a, b):
        return jnp.dot(lhs, w_ref[:, a:b], preferred_element_type=F32)

    def conv_proj(cb):
        return proj(xe, OFF_MQK + cb * cw, OFF_MQK + (cb + 1) * cw)

    def conv_finish(cb, pm):
        acc = bc_ref[:, cb * cw:(cb + 1) * cw]
        for tap in range(CONV_WIDTH):
            shift = (CONV_WIDTH // 2 - tap) % ext
            shifted = pm if shift == 0 else pltpu.roll(pm, shift, 0)
            acc = acc + shifted[SUBLANES:SUBLANES + tm, :] * wc_ref[tap:tap + 1, cb * cw:(cb + 1) * cw]
        y = acc * jax.nn.sigmoid(acc)
        if cb < M_WIDTH // cw:
            y = y * (M_HEAD_DIM ** -0.5)
        mqk[:, cb * cw:(cb + 1) * cw] = y.astype(BF16)
        if cb >= M_WIDTH // cw:
            r0 = cb * cw - M_WIDTH
            mkt[r0:r0 + cw, :] = y.T.astype(BF16)

    def permuted_qk(g, ref):
        d = DILATIONS[g]
        n = tm // d
        xperm = jnp.concatenate([_slab_rows(xs, c, n, d) for c in range(d)], axis=0)
        pg = proj(xperm.astype(BF16), OFF_QK + g * 2 * ATTN_W, OFF_QK + (g + 1) * 2 * ATTN_W).astype(BF16)
        for c in range(d):
            ref[:, c * 2 * ATTN_W:(c + 1) * 2 * ATTN_W] = pg[c * n:(c + 1) * n, :]

    pm = conv_proj(0)
    qk0[...] = proj(xb, OFF_QK, OFF_QK + 2 * ATTN_W).astype(BF16)
    conv_finish(0, pm)
    pm = conv_proj(1)
    permuted_qk(1, qk1)
    conv_finish(1, pm)
    pm = conv_proj(2)
    permuted_qk(2, qk2)
    conv_finish(2, pm)
    pm = conv_proj(3)
    vf = proj(xb, OFF_V, OFF_V + ATTN_W)
    v0[...] = vf.astype(BF16)
    _slab_store(vs, 0, vf)
    for g, ref in ((1, v1), (2, v2)):
        d = DILATIONS[g]
        for c in range(d):
            ref[:, c * ATTN_W:(c + 1) * ATTN_W] = _slab_rows(vs, c, tm // d, d).astype(BF16)
    vm = proj(xb, OFF_MV, OFF_MV + M_WIDTH)
    mv[...] = vm.astype(BF16)
    mvt[...] = vm.T.astype(BF16)
    conv_finish(3, pm)
    mo[...] = proj(xb, OFF_MO, OFF_MO + M_WIDTH).astype(BF16)
    bg[...] = proj(xb, OFF_BG, OFF_BG + 2 * D_MODEL).astype(BF16)
    gp = lax.dot_general(wg_ref[...], xb, _NT, preferred_element_type=F32) + bias_ref[...]
    row = lax.broadcasted_iota(jnp.int32, gp.shape, 0)
    log_sig = jnp.minimum(gp, 0.0) - jnp.log1p(jnp.exp(-jnp.abs(gp)))
    gates[...] = jnp.where(row % 2 == 1, log_sig, gp)


def _inproj(x, p, s, tm):
    t = x.shape[0]
    hb = tm // SUBLANES
    n_halo = t // SUBLANES
    row = lambda w: pl.BlockSpec((tm, w), lambda i: (i, 0))
    view = lambda d, w: pl.BlockSpec((tm // d, d * w), lambda i: (i, 0))
    out_specs = (row(2 * ATTN_W), view(4, 2 * ATTN_W), view(16, 2 * ATTN_W),
                 row(ATTN_W), view(4, ATTN_W), view(16, ATTN_W),
                 row(2 * M_WIDTH), row(M_WIDTH), row(M_WIDTH), row(2 * D_MODEL),
                 pl.BlockSpec((GATE_ROWS, tm), lambda i: (0, i)),
                 pl.BlockSpec((M_WIDTH, tm), lambda i: (0, i)), pl.BlockSpec((M_WIDTH, tm), lambda i: (0, i)))
    bshape = lambda d, w: jax.ShapeDtypeStruct((t // d, d * w), BF16)
    out_shape = (bshape(1, 2 * ATTN_W), bshape(4, 2 * ATTN_W), bshape(16, 2 * ATTN_W),
                 bshape(1, ATTN_W), bshape(4, ATTN_W), bshape(16, ATTN_W),
                 bshape(1, 2 * M_WIDTH), bshape(1, M_WIDTH), bshape(1, M_WIDTH), bshape(1, 2 * D_MODEL),
                 jax.ShapeDtypeStruct((GATE_ROWS, t), F32),
                 jax.ShapeDtypeStruct((M_WIDTH, t), BF16), jax.ShapeDtypeStruct((M_WIDTH, t), BF16))
    return pl.pallas_call(
        functools.partial(_inproj_kernel, tm=tm, tiles_per_seq=s // tm),
        grid=(t // tm,),
        in_specs=[row(D_MODEL),
                  pl.BlockSpec((SUBLANES, D_MODEL), lambda i: (jnp.maximum(i * hb - 1, 0), 0)),
                  pl.BlockSpec((SUBLANES, D_MODEL), lambda i: (jnp.minimum((i + 1) * hb, n_halo - 1), 0)),
                  _resident((1, D_MODEL)), _resident((D_MODEL, N_MAIN)), _resident((GATE_ROWS, D_MODEL)),
                  _resident((GATE_ROWS, 1)), _resident((CONV_WIDTH, 2 * M_WIDTH)), _resident((1, 2 * M_WIDTH))],
        out_specs=out_specs,
        out_shape=out_shape,
        scratch_shapes=[pltpu.VMEM((D_MODEL // LANES, tm, LANES), F32), pltpu.VMEM((ATTN_W // LANES, tm, LANES), F32)],
        compiler_params=_params("parallel"),
        name="inproj",
    )(x, x, x, p["norm_mix_pre"], p["w_main"], p["w_gate"], p["b_gate"], p["w_conv"], p["b_conv"])


ATTN_TILE = 128
ATTN_KEYS = ATTN_TILE + 2 * HALF_WIN
ATTN_BLOCK = 1024
PAIR_W = 2 * HEAD_DIM


def _attn_init(bias, vw, *, slopes, dilation):
    qi = lax.broadcasted_iota(jnp.int32, (ATTN_TILE, ATTN_KEYS), 0)
    kj = lax.broadcasted_iota(jnp.int32, (ATTN_TILE, ATTN_KEYS), 1)
    step = jnp.abs(kj - HALF_WIN - qi)
    dist = (step * dilation).astype(F32)
    band = jnp.where(step <= HALF_WIN, 0.0, NEG)
    for variant in range(4):
        mask = band
        if variant & 1:
            mask = jnp.where(kj < HALF_WIN, NEG, mask)
        if variant & 2:
            mask = jnp.where(kj >= ATTN_TILE + HALF_WIN, NEG, mask)
        for h in range(HEADS_PER_GROUP):
            bias[variant, h] = -float(slopes[h]) * dist + mask
    ones = jnp.ones((vw.shape[0], PAIR_W), BF16)
    for pair in range(HEADS_PER_GROUP // 2):
        vw[:, (2 * pair + 1) * PAIR_W:(2 * pair + 2) * PAIR_W] = ones


def _attn_kernel(q_ref, k_ref, kp_ref, kn_ref, v_ref, vp_ref, vn_ref, o_ref, lse_ref, kw, vw, bias,
                 *, slopes, dilation, tqb, nq):
    i = pl.program_id(2)

    @pl.when((pl.program_id(0) == 0) & (pl.program_id(1) == 0) & (i == 0))
    def _():
        _attn_init(bias, vw, slopes=slopes, dilation=dilation)

    kw[0:HALF_WIN, :] = kp_ref[...]
    kw[HALF_WIN:HALF_WIN + tqb, :] = k_ref[...]
    kw[HALF_WIN + tqb:, :] = kn_ref[...]
    for pair in range(HEADS_PER_GROUP // 2):
        src = slice(pair * PAIR_W, (pair + 1) * PAIR_W)
        dst = slice(2 * pair * PAIR_W, (2 * pair + 1) * PAIR_W)
        vw[0:HALF_WIN, dst] = vp_ref[:, src]
        vw[HALF_WIN:HALF_WIN + tqb, dst] = v_ref[:, src]
        vw[HALF_WIN + tqb:, dst] = vn_ref[:, src]

    lane = lax.broadcasted_iota(jnp.int32, (ATTN_TILE, PAIR_W), 1)
    low = lane < HEAD_DIM
    n_tiles = tqb // ATTN_TILE
    for t in range(n_tiles):
        variant = 0
        if t == 0:
            variant = variant + jnp.where(i == 0, 1, 0)
        if t == n_tiles - 1:
            variant = variant + jnp.where(i == nq - 1, 2, 0)
        rows = slice(t * ATTN_TILE, (t + 1) * ATTN_TILE)
        krows = slice(t * ATTN_TILE, t * ATTN_TILE + ATTN_KEYS)
        heads = range(HEADS_PER_GROUP)
        pcols = [slice((h // 2) * PAIR_W, (h // 2 + 1) * PAIR_W) for h in heads]
        scores = []
        for h in heads:
            q2 = q_ref[rows, pcols[h]]
            qh = jnp.where(low if h % 2 == 0 else ~low, q2, jnp.zeros_like(q2))
            scores.append(lax.dot_general(qh, kw[krows, pcols[h]], _NT, preferred_element_type=F32))
        probs, maxes = [], []
        for h in heads:
            s = scores[h] + bias[variant, h]
            m = jnp.max(s, axis=-1, keepdims=True)
            probs.append(jnp.exp(s - m).astype(BF16))
            maxes.append(m)
        pvs = [jnp.dot(probs[h], vw[krows, (h // 2) * 2 * PAIR_W:(h // 2 + 1) * 2 * PAIR_W],
                       preferred_element_type=F32) for h in heads]
        lse = jnp.zeros((ATTN_TILE, PAIR_W), F32)
        outs = []
        for h in heads:
            l = pvs[h][:, PAIR_W:]
            outs.append(pvs[h][:, :PAIR_W] * (1.0 / l))
            lse = jnp.where(lane == h, maxes[h] + jnp.log(l), lse)
        lse_ref[rows, :] = lse[:, 0:HEADS_PER_GROUP]
        for pair in range(HEADS_PER_GROUP // 2):
            o_ref[rows, pcols[2 * pair]] = jnp.where(low, outs[2 * pair], outs[2 * pair + 1]).astype(o_ref.dtype)


def _attention_group(qk, v, g, nb, s):
    dilation = DILATIONS[g]
    rows = qk.shape[0]
    sub_len = s // dilation
    tqb = min(ATTN_BLOCK, sub_len)
    nq = sub_len // tqb
    hb = tqb // HALF_WIN
    n_halo = rows // HALF_WIN

    def spec(size, row_fn, col_fn):
        return pl.BlockSpec((size, ATTN_W), lambda b, c, i: (row_fn(b * nq + i), col_fn(c)))

    cur = lambda r: r
    prv = lambda r: jnp.maximum(r * hb - 1, 0)
    nxt = lambda r: jnp.minimum((r + 1) * hb, n_halo - 1)
    qcol, kcol, vcol = (lambda c: 2 * c), (lambda c: 2 * c + 1), (lambda c: c)
    kern = functools.partial(_attn_kernel, slopes=tuple(_alibi_slopes()[g]), dilation=dilation, tqb=tqb, nq=nq)
    return pl.pallas_call(
        kern,
        grid=(nb, dilation, nq),
        in_specs=[spec(tqb, cur, qcol), spec(tqb, cur, kcol), spec(HALF_WIN, prv, kcol), spec(HALF_WIN, nxt, kcol),
                  spec(tqb, cur, vcol), spec(HALF_WIN, prv, vcol), spec(HALF_WIN, nxt, vcol)],
        out_specs=(spec(tqb, cur, vcol),
                   pl.BlockSpec((None, tqb, HEADS_PER_GROUP), lambda b, c, i: (c, b * nq + i, 0))),
        out_shape=(jax.ShapeDtypeStruct((rows, dilation * ATTN_W), BF16),
                   jax.ShapeDtypeStruct((dilation, rows, HEADS_PER_GROUP), F32)),
        scratch_shapes=[pltpu.VMEM((tqb + 2 * HALF_WIN, ATTN_W), BF16),
                        pltpu.VMEM((tqb + 2 * HALF_WIN, 2 * ATTN_W), BF16),
                        pltpu.VMEM((4, HEADS_PER_GROUP, ATTN_TILE, ATTN_KEYS), F32)],
        compiler_params=_params("arbitrary", "arbitrary", "arbitrary"),
        name=f"attn_g{g}",
    )(qk, qk, qk, qk, v, v, v)


MLSTM_CHUNKS = 16


def _mlstm_stage1a(q, k, li_row, lf_row, tri):
    return dict(qk=lax.dot_general(q, k, _NT, preferred_element_type=F32), li_row=li_row,
                cum_col=jnp.sum(jnp.where(tri, lf_row, 0.0), axis=1, keepdims=True),
                tot=jnp.sum(lf_row, axis=1, keepdims=True))


def _mlstm_stage1b(rec, tri, eye):
    cum_row = jnp.sum(jnp.where(eye, rec["cum_col"], 0.0), axis=0, keepdims=True)
    li_row = rec.pop("li_row")
    dmat = jnp.where(tri, rec["cum_col"] - cum_row + li_row, NEG)
    a = rec["tot"] - cum_row + li_row
    rec.update(dmat=dmat, a=a, m_loc=jnp.max(dmat, axis=1, keepdims=True), a_max=jnp.max(a, axis=1, keepdims=True))


def _mlstm_stage1c(rec, m_prev):
    m_inter = rec["cum_col"] + m_prev
    m_t = jnp.maximum(m_inter, rec.pop("m_loc"))
    m_new = jnp.maximum(rec["tot"] + m_prev, rec.pop("a_max"))
    rec.update(p=jnp.exp(rec.pop("dmat") - m_t), w_k=jnp.exp(rec.pop("a") - m_new),
               w_inter=jnp.exp(m_inter - m_t), floor=jnp.exp(-m_t), decay=jnp.exp(rec["tot"] + m_prev - m_new))
    return m_new


def _mlstm_stage2(rec, kt):
    sc = rec["qk"] * rec["p"]
    w_k = rec["w_k"]
    rec.update(sc=sc.astype(BF16), den=jnp.sum(sc, axis=1, keepdims=True),
               kwt=(kt.astype(F32) * w_k).astype(BF16),
               w_rows=jnp.broadcast_to(w_k, (SUBLANES, w_k.shape[1])).astype(BF16))


def _mlstm_stage3(rec, k, v, vt):
    rec.update(num=jnp.dot(rec["sc"], v, preferred_element_type=F32),
               upd=lax.dot_general(rec["kwt"], vt, _NT, preferred_element_type=F32),
               n_upd=jnp.dot(rec["w_rows"], k, preferred_element_type=F32)[0:1, :])


def _mlstm_stage4a(rec, state):
    c_prev, n_prev = state
    rec.update(c_in=c_prev.astype(BF16), n_in=n_prev)
    decay = rec.pop("decay")
    return decay * c_prev + rec.pop("upd"), decay * n_prev + rec.pop("n_upd")


def _mlstm_stage4b(rec, q):
    rec.update(qc=jnp.dot(q, rec.pop("c_in"), preferred_element_type=F32),
               qn=jnp.sum(q.astype(F32) * rec.pop("n_in"), axis=1, keepdims=True))


def _mlstm_stage4c(rec):
    w_inter = rec["w_inter"]
    den = rec["den"] + w_inter * rec["qn"]
    return (rec["num"] + w_inter * rec["qc"]) * (1.0 / jnp.maximum(jnp.abs(den), rec["floor"]))


def _mlstm_kernel(qf, kf, vf, ktf, vtf, gf, qb, kb, vb, ktb, vtb, gb, hf_ref, hb_ref, c_scr, n_scr, m_scr):
    @pl.when(pl.program_id(2) == 0)
    def _():
        c_scr[...] = jnp.zeros_like(c_scr)
        n_scr[...] = jnp.zeros_like(n_scr)
        m_scr[...] = jnp.zeros_like(m_scr)

    L = M_CHUNK
    ti = lax.broadcasted_iota(jnp.int32, (L, L), 0)
    si = lax.broadcasted_iota(jnp.int32, (L, L), 1)
    eye = ti == si
    f_rows = [slice(c * L, (c + 1) * L) for c in range(MLSTM_CHUNKS)]
    dirs = (((qf, kf, vf, ktf, vtf, gf), (0, 1), si <= ti, f_rows, hf_ref),
            ((qb, kb, vb, ktb, vtb, gb), (2, 3), si >= ti, f_rows[::-1], hb_ref))
    recs = [[], []]
    for c in range(MLSTM_CHUNKS):
        for d, ((q, k, _, _, _, g), (gi, gf_), tri, rows, _) in enumerate(dirs):
            r = rows[c]
            recs[d].append(_mlstm_stage1a(q[r, :], k[r, :], g[gi:gi + 1, r], g[gf_:gf_ + 1, r], tri))
    for c in range(MLSTM_CHUNKS):
        for d in range(2):
            _mlstm_stage1b(recs[d][c], dirs[d][2], eye)
    m = [m_scr[0], m_scr[1]]
    for c in range(MLSTM_CHUNKS):
        for d in range(2):
            m[d] = _mlstm_stage1c(recs[d][c], m[d])
    for c in range(MLSTM_CHUNKS):
        for d, ((_, _, _, kt, _, _), _, _, rows, _) in enumerate(dirs):
            _mlstm_stage2(recs[d][c], kt[:, rows[c]])
    for c in range(MLSTM_CHUNKS):
        for d, ((_, k, v, _, vt, _), _, _, rows, _) in enumerate(dirs):
            _mlstm_stage3(recs[d][c], k[rows[c], :], v[rows[c], :], vt[:, rows[c]])
    state = [(c_scr[0], n_scr[0]), (c_scr[1], n_scr[1])]
    for c in range(MLSTM_CHUNKS):
        for d in range(2):
            state[d] = _mlstm_stage4a(recs[d][c], state[d])
    for c in range(MLSTM_CHUNKS):
        for d, ((q, _, _, _, _, _), _, _, rows, _) in enumerate(dirs):
            _mlstm_stage4b(recs[d][c], q[rows[c], :])
    for c in range(MLSTM_CHUNKS):
        for d, (_, _, _, rows, h_ref) in enumerate(dirs):
            h_ref[rows[c], :] = _mlstm_stage4c(recs[d][c]).astype(h_ref.dtype)
    for d in range(2):
        c_scr[d] = state[d][0]
        n_scr[d] = state[d][1]
        m_scr[d] = m[d]


def _mlstm(qk, v, kt, vt, gates, nb, s):
    t = v.shape[0]
    rows = MLSTM_CHUNKS * M_CHUNK
    nr = s // rows
    fblk = lambda b, j: b * nr + j
    bblk = lambda b, j: b * nr + (nr - 1 - j)

    def specs(blk):
        return [pl.BlockSpec((rows, M_HEAD_DIM), lambda b, h, j: (blk(b, j), h)),
                pl.BlockSpec((rows, M_HEAD_DIM), lambda b, h, j: (blk(b, j), M_HEADS + h)),
                pl.BlockSpec((rows, M_HEAD_DIM), lambda b, h, j: (blk(b, j), h)),
                pl.BlockSpec((M_HEAD_DIM, rows), lambda b, h, j: (h, blk(b, j))),
                pl.BlockSpec((M_HEAD_DIM, rows), lambda b, h, j: (h, blk(b, j))),
                pl.BlockSpec((SUBLANES, rows), lambda b, h, j: (h, blk(b, j)))]

    out = lambda blk: pl.BlockSpec((rows, M_HEAD_DIM), lambda b, h, j: (blk(b, j), h))
    return pl.pallas_call(
        _mlstm_kernel,
        grid=(nb, M_HEADS, nr),
        in_specs=specs(fblk) + specs(bblk),
        out_specs=(out(fblk), out(bblk)),
        out_shape=(jax.ShapeDtypeStruct((t, M_WIDTH), BF16), jax.ShapeDtypeStruct((t, M_WIDTH), BF16)),
        scratch_shapes=[pltpu.VMEM((2, M_HEAD_DIM, M_HEAD_DIM), F32), pltpu.VMEM((2, 1, M_HEAD_DIM), F32),
                        pltpu.VMEM((2, 1, 1), F32)],
        compiler_params=_params("parallel", "parallel", "arbitrary"),
        name="mlstm",
    )(qk, qk, v, kt, vt, gates, qk, qk, v, kt, vt, gates)


LSE_PAD = 128


def _expand_matrix():
    r = np.arange(LSE_PAD)[:, None]
    c = np.arange(N_GROUPS * ATTN_W)[None, :]
    hit = (r < N_GROUPS * 2 * HEADS_PER_GROUP) & (c // ATTN_W == r // (2 * HEADS_PER_GROUP)) \
        & ((c % ATTN_W) // HEAD_DIM == r % HEADS_PER_GROUP)
    return jnp.asarray(hit, dtype=BF16)


def _mix_out_kernel(o0, o1, o2, l0, l1, l2, hf, hb, mo, bg, x_ref, wa, wm, wo, gn, gpost, expand, y_ref,
                    wpad, os1, os2, *, tm):
    hs = hf[...].astype(F32) + hb[...].astype(F32)
    parts = []
    for h in range(M_HEADS):
        p = hs[:, h * M_HEAD_DIM:(h + 1) * M_HEAD_DIM]
        parts.append(p * lax.rsqrt(jnp.mean(p * p, axis=-1, keepdims=True) + EPS))
    hn = jnp.concatenate(parts, axis=-1) * gn[...]
    ml = jax.nn.sigmoid(mo[...].astype(F32)) * hn
    ml_proj = jnp.dot(ml.astype(BF16), wm[...], preferred_element_type=F32)
    lses = [l0[...], l1[...], l2[...]]
    mx = jnp.maximum(jnp.maximum(lses[0], lses[1]), lses[2])
    es = [jnp.exp(l - mx) for l in lses]
    inv_z = 1.0 / (es[0] + es[1] + es[2])
    wpad[...] = jnp.zeros_like(wpad)
    for g, e in enumerate(es):
        w = e * inv_z
        w_hi = w.astype(BF16).astype(F32)
        base = g * 2 * HEADS_PER_GROUP
        wpad[:, base:base + HEADS_PER_GROUP] = w_hi
        wpad[:, base + HEADS_PER_GROUP:base + 2 * HEADS_PER_GROUP] = w - w_hi
    wide = jnp.dot(wpad[...].astype(BF16), expand[...], preferred_element_type=F32)
    for o, osc, d in ((o1, os1, DILATIONS[1]), (o2, os2, DILATIONS[2])):
        n = tm // d
        for c in range(d):
            for j in range(ATTN_W // LANES):
                lo = c * ATTN_W + j * LANES
                osc[j, pl.ds(c, n, stride=d), :] = o[:, lo:lo + LANES].astype(F32)
    outs = [o0[...].astype(F32)] + [jnp.concatenate([osc[j] for j in range(ATTN_W // LANES)], axis=1)
                                    for osc in (os1, os2)]
    attn = (wide[:, :ATTN_W] * outs[0] + wide[:, ATTN_W:2 * ATTN_W] * outs[1] + wide[:, 2 * ATTN_W:] * outs[2])
    attn_proj = jnp.dot(attn.astype(BF16), wa[...], preferred_element_type=F32)
    gates = jax.nn.sigmoid(bg[...].astype(F32))
    merged = gates[:, :D_MODEL] * attn_proj + gates[:, D_MODEL:] * ml_proj
    mix = jnp.dot(merged.astype(BF16), wo[...], preferred_element_type=F32)
    y_ref[...] = x_ref[...] + _rms(mix, gpost[...])


def _mix_out(outs, lses, hf, hb, mo, bg, x, p, tm):
    t = x.shape[0]
    row = lambda w: pl.BlockSpec((tm, w), lambda i: (i, 0))
    view = lambda d: pl.BlockSpec((tm // d, d * ATTN_W), lambda i: (i, 0))
    return pl.pallas_call(
        functools.partial(_mix_out_kernel, tm=tm),
        grid=(t // tm,),
        in_specs=[row(ATTN_W), view(4), view(16),
                  row(HEADS_PER_GROUP), row(HEADS_PER_GROUP), row(HEADS_PER_GROUP),
                  row(M_WIDTH), row(M_WIDTH), row(M_WIDTH), row(2 * D_MODEL), row(D_MODEL),
                  _resident((ATTN_W, D_MODEL)), _resident((M_WIDTH, D_MODEL)), _resident((D_MODEL, D_MODEL)),
                  _resident((1, M_WIDTH)), _resident((1, D_MODEL)), _resident((LSE_PAD, N_GROUPS * ATTN_W))],
        out_specs=row(D_MODEL),
        out_shape=jax.ShapeDtypeStruct((t, D_MODEL), F32),
        scratch_shapes=[pltpu.VMEM((tm, LSE_PAD), F32),
                        pltpu.VMEM((ATTN_W // LANES, tm, LANES), F32),
                        pltpu.VMEM((ATTN_W // LANES, tm, LANES), F32)],
        compiler_params=_params("parallel"),
        name="mix_out",
    )(*outs, *lses, hf, hb, mo, bg, x, p["w_attn_proj"], p["w_mlstm_proj"], p["w_out"],
      p["g_mlstm_norm"], p["norm_mix_post"], _expand_matrix())


MXU_TILE = 256
FFN_SPLITS = (0, 6 * MXU_TILE, FFN_HIDDEN)
assert FFN_HIDDEN % MXU_TILE == 0


def _ffn_kernel(x_ref, gpre, win, wout, gpost, y_ref):
    x = x_ref[...]
    xb = _rms(x, gpre[...]).astype(BF16)
    acc = jnp.zeros(x.shape, F32)
    for lo, hi in zip(FFN_SPLITS[:-1], FFN_SPLITS[1:]):
        gate = jnp.dot(xb, win[:, lo:hi], preferred_element_type=F32)
        up = jnp.dot(xb, win[:, FFN_HIDDEN + lo:FFN_HIDDEN + hi], preferred_element_type=F32)
        act = (gate * jax.nn.sigmoid(gate) * up).astype(BF16)
        acc = acc + jnp.dot(act, wout[lo:hi, :], preferred_element_type=F32)
    y_ref[...] = x + _rms(acc, gpost[...])


def _ffn(x, p, tm):
    t = x.shape[0]
    row = pl.BlockSpec((tm, D_MODEL), lambda i: (i, 0))
    return pl.pallas_call(
        _ffn_kernel,
        grid=(t // tm,),
        in_specs=[row, _resident((1, D_MODEL)), _resident((D_MODEL, 2 * FFN_HIDDEN)),
                  _resident((FFN_HIDDEN, D_MODEL)), _resident((1, D_MODEL))],
        out_specs=row,
        out_shape=jax.ShapeDtypeStruct((t, D_MODEL), F32),
        compiler_params=_params("parallel"),
        name="ffn",
    )(x, p["norm_ffn_pre"], p["w_ffn_in"], p["w_ffn_out"], p["norm_ffn_post"])


TM_PROJ = 256
TM_MIX = 512
TM_FFN = 1024


def _layer(x, nb, s, p):
    qk0, qk1, qk2, v0, v1, v2, mqk, mv, mo, bg, gates, mkt, mvt = _inproj(x, p, s, TM_PROJ)
    attn = [_attention_group(qk, v, g, nb, s) for g, (qk, v) in enumerate(((qk0, v0), (qk1, v1), (qk2, v2)))]
    hf, hb = _mlstm(mqk, mv, mkt, mvt, gates, nb, s)
    lses = [a[1].transpose(1, 0, 2).reshape(x.shape[0], HEADS_PER_GROUP) for a in attn]
    x = _mix_out([a[0] for a in attn], lses, hf, hb, mo, bg, x, p, TM_MIX)
    return _ffn(x, p, TM_FFN)


def _prepare(norm_mix_pre, norm_mix_post, norm_ffn_pre, norm_ffn_post, w_in, b_mlstm_gates, w_conv, b_conv,
             g_mlstm_norm, w_attn_proj, w_mlstm_proj, w_out, w_ffn_in, w_ffn_out):
    depth = w_in.shape[0]
    sizes = (3 * ATTN_W, 3 * ATTN_W, ATTN_W, 2 * M_WIDTH, M_WIDTH, M_WIDTH, 4 * M_HEADS, 2 * D_MODEL)
    offs = [int(c) for c in np.cumsum((0,) + sizes)]
    aq, ak, av, mqk, mv, mo, mg, bg = (w_in[:, :, offs[j]:offs[j + 1]] for j in range(8))
    aq = aq * (HEAD_DIM ** -0.5)
    qk = [jnp.concatenate([aq[:, :, g * ATTN_W:(g + 1) * ATTN_W], ak[:, :, g * ATTN_W:(g + 1) * ATTN_W]], axis=-1)
          for g in range(N_GROUPS)]
    w_main = jnp.concatenate(qk + [av, mqk, mv, mo, bg], axis=-1).astype(BF16)
    wg = mg.reshape(depth, D_MODEL, 4, M_HEADS).transpose(0, 3, 2, 1)
    wg = jnp.pad(wg, ((0, 0), (0, 0), (0, SUBLANES - 4), (0, 0))).reshape(depth, GATE_ROWS, D_MODEL).astype(BF16)
    bgate = b_mlstm_gates.reshape(depth, 4, M_HEADS).transpose(0, 2, 1)
    bgate = jnp.pad(bgate, ((0, 0), (0, 0), (0, SUBLANES - 4))).reshape(depth, GATE_ROWS, 1)
    layers = []
    for l in range(depth):
        layers.append(dict(
            norm_mix_pre=norm_mix_pre[l][None], norm_mix_post=norm_mix_post[l][None],
            norm_ffn_pre=norm_ffn_pre[l][None], norm_ffn_post=norm_ffn_post[l][None],
            w_main=w_main[l], w_gate=wg[l], b_gate=bgate[l],
            w_conv=w_conv[l], b_conv=b_conv[l][None], g_mlstm_norm=g_mlstm_norm[l][None],
            w_attn_proj=w_attn_proj[l].astype(BF16), w_mlstm_proj=w_mlstm_proj[l].astype(BF16),
            w_out=w_out[l].astype(BF16), w_ffn_in=w_ffn_in[l].astype(BF16), w_ffn_out=w_ffn_out[l].astype(BF16)))
    return layers


def _trunk(x, layers):
    nb, s, d = x.shape
    y = x.reshape(nb * s, d)
    for p in layers:
        y = _layer(y, nb, s, p)
    return y.reshape(nb, s, d)


def kernel(x_prompt, x_sample, norm_mix_pre, norm_mix_post, norm_ffn_pre, norm_ffn_post, w_in, b_mlstm_gates,
           w_conv, b_conv, g_mlstm_norm, w_attn_proj, w_mlstm_proj, w_out, w_ffn_in, w_ffn_out):
    layers = _prepare(norm_mix_pre, norm_mix_post, norm_ffn_pre, norm_ffn_post, w_in, b_mlstm_gates, w_conv,
                      b_conv, g_mlstm_norm, w_attn_proj, w_mlstm_proj, w_out, w_ffn_in, w_ffn_out)
    return _trunk(x_prompt, layers), _trunk(x_sample, layers)
```

```python
import functools

import numpy as np
import jax
import jax.numpy as jnp
from jax import lax
from jax.experimental import pallas as pl
from jax.experimental.pallas import tpu as pltpu

F32 = jnp.float32
BF16 = jnp.bfloat16

D_MODEL = 1024
ATTN_GROUPS = ((128, 1), (512, 4), (2048, 16))
DILATIONS = tuple(d for _, d in ATTN_GROUPS)
N_GROUPS = 3
HEADS_PER_GROUP = 8
HEAD_DIM = 64
HALF_WIN = 64
ATTN_W = HEADS_PER_GROUP * HEAD_DIM
M_WIDTH = 1024
M_HEADS = 4
M_HEAD_DIM = 256
M_CHUNK = 128
CONV_WIDTH = 5
FFN_HIDDEN = 2816
EPS = 1e-6
NEG = -1e30
SUBLANES = 8
GATE_ROWS = M_HEADS * SUBLANES

OFF_QK = 0
OFF_V = OFF_QK + 2 * N_GROUPS * ATTN_W
OFF_MQK = OFF_V + ATTN_W
OFF_MV = OFF_MQK + 2 * M_WIDTH
OFF_MO = OFF_MV + M_WIDTH
OFF_BG = OFF_MO + M_WIDTH
N_MAIN = OFF_BG + 2 * D_MODEL

VMEM_LIMIT = 56 * 1024 * 1024


def _alibi_slopes():
    h = np.arange(1, N_GROUPS * HEADS_PER_GROUP + 1, dtype=np.float32)
    return (2.0 ** (-8.0 * h / (N_GROUPS * HEADS_PER_GROUP))).astype(np.float32).reshape(N_GROUPS, HEADS_PER_GROUP)


def _rms(x, gain):
    return x * lax.rsqrt(jnp.mean(x * x, axis=-1, keepdims=True) + EPS) * gain


def _resident(shape):
    nd = len(shape)
    return pl.BlockSpec(shape, lambda *_: (0,) * nd, pipeline_mode=pl.Buffered(1))


_NT = (((1,), (1,)), ((), ()))
LANES = 128


def _slab_store(ref, row0, value):
    rows = value.shape[0]
    for j in range(ref.shape[0]):
        ref[j, row0:row0 + rows, :] = value[:, j * LANES:(j + 1) * LANES]


def _slab_rows(ref, start, size, stride):
    return jnp.concatenate([ref[j, pl.ds(start, size, stride=stride), :] for j in range(ref.shape[0])], axis=1)


def _params(*sem):
    return pltpu.CompilerParams(dimension_semantics=sem, vmem_limit_bytes=VMEM_LIMIT)


def _inproj_kernel(x_ref, xp_ref, xn_ref, gain_ref, w_ref, wg_ref, bias_ref, wc_ref, bc_ref,
                   qk0, qk1, qk2, v0, v1, v2, mqk, mv, mo, bg, gates, mkt, mvt, xs, vs, *, tm, tiles_per_seq):
    i = pl.program_id(0)
    gain = gain_ref[...]
    first = (i % tiles_per_seq) == 0
    last = (i % tiles_per_seq) == tiles_per_seq - 1
    xn = _rms(x_ref[...], gain)
    x_prev = jnp.where(first, 0.0, _rms(xp_ref[...], gain))
    x_next = jnp.where(last, 0.0, _rms(xn_ref[...], gain))
    _slab_store(xs, 0, xn)
    xb = xn.astype(BF16)
    xe = jnp.concatenate([x_prev, xn, x_next], axis=0).astype(BF16)
    ext = tm + 2 * SUBLANES
    cw = ATTN_W

    def proj(lhs, a, b):
        return jnp.dot(lhs, w_ref[:, a:b], preferred_element_type=F32)

    def conv_proj(cb):
        return proj(xe, OFF_MQK + cb * cw, OFF_MQK + (cb + 1) * cw)

    def conv_finish(cb, pm):
        acc = bc_ref[:, cb * cw:(cb + 1) * cw]
        for tap in range(CONV_WIDTH):
            shift = (CONV_WIDTH // 2 - tap) % ext
            shifted = pm if shift == 0 else pltpu.roll(pm, shift, 0)
            acc = acc + shifted[SUBLANES:SUBLANES + tm, :] * wc_ref[tap:tap + 1, cb * cw:(cb + 1) * cw]
        y = acc * jax.nn.sigmoid(acc)
        if cb < M_WIDTH // cw:
            y = y * (M_HEAD_DIM ** -0.5)
        mqk[:, cb * cw:(cb + 1) * cw] = y.astype(BF16)
        if cb >= M_WIDTH // cw:
            r0 = cb * cw - M_WIDTH
            mkt[r0:r0 + cw, :] = y.T.astype(BF16)

    def permuted_qk(g, ref):
        d = DILATIONS[g]
        n = tm // d
        xperm = jnp.concatenate([_slab_rows(xs, c, n, d) for c in range(d)], axis=0)
        pg = proj(xperm.astype(BF16), OFF_QK + g * 2 * ATTN_W, OFF_QK + (g + 1) * 2 * ATTN_W).astype(BF16)
        for c in range(d):
            ref[:, c * 2 * ATTN_W:(c + 1) * 2 * ATTN_W] = pg[c * n:(c + 1) * n, :]

    pm = conv_proj(0)
    qk0[...] = proj(xb, OFF_QK, OFF_QK + 2 * ATTN_W).astype(BF16)
    conv_finish(0, pm)
    pm = conv_proj(1)
    permuted_qk(1, qk1)
    conv_finish(1, pm)
    pm = conv_proj(2)
    permuted_qk(2, qk2)
    conv_finish(2, pm)
    pm = conv_proj(3)
    vf = proj(xb, OFF_V, OFF_V + ATTN_W)
    v0[...] = vf.astype(BF16)
    _slab_store(vs, 0, vf)
    for g, ref in ((1, v1), (2, v2)):
        d = DILATIONS[g]
        for c in range(d):
            ref[:, c * ATTN_W:(c + 1) * ATTN_W] = _slab_rows(vs, c, tm // d, d).astype(BF16)
    vm = proj(xb, OFF_MV, OFF_MV + M_WIDTH)
    mv[...] = vm.astype(BF16)
    mvt[...] = vm.T.astype(BF16)
    conv_finish(3, pm)
    mo[...] = jax.nn.sigmoid(proj(xb, OFF_MO, OFF_MO + M_WIDTH)).astype(BF16)
    bg[...] = jax.nn.sigmoid(proj(xb, OFF_BG, OFF_BG + 2 * D_MODEL)).astype(BF16)
    gp = lax.dot_general(wg_ref[...], xb, _NT, preferred_element_type=F32) + bias_ref[...]
    row = lax.broadcasted_iota(jnp.int32, gp.shape, 0)
    log_sig = jnp.minimum(gp, 0.0) - jnp.log1p(jnp.exp(-jnp.abs(gp)))
    gates[...] = jnp.where(row % 2 == 1, log_sig, gp)


def _inproj(x, p, s, tm):
    t = x.shape[0]
    hb = tm // SUBLANES
    n_halo = t // SUBLANES
    row = lambda w: pl.BlockSpec((tm, w), lambda i: (i, 0))
    view = lambda d, w: pl.BlockSpec((tm // d, d * w), lambda i: (i, 0))
    out_specs = (row(2 * ATTN_W), view(4, 2 * ATTN_W), view(16, 2 * ATTN_W),
                 row(ATTN_W), view(4, ATTN_W), view(16, ATTN_W),
                 row(2 * M_WIDTH), row(M_WIDTH), row(M_WIDTH), row(2 * D_MODEL),
                 pl.BlockSpec((GATE_ROWS, tm), lambda i: (0, i)),
                 pl.BlockSpec((M_WIDTH, tm), lambda i: (0, i)), pl.BlockSpec((M_WIDTH, tm), lambda i: (0, i)))
    bshape = lambda d, w: jax.ShapeDtypeStruct((t // d, d * w), BF16)
    out_shape = (bshape(1, 2 * ATTN_W), bshape(4, 2 * ATTN_W), bshape(16, 2 * ATTN_W),
                 bshape(1, ATTN_W), bshape(4, ATTN_W), bshape(16, ATTN_W),
                 bshape(1, 2 * M_WIDTH), bshape(1, M_WIDTH), bshape(1, M_WIDTH), bshape(1, 2 * D_MODEL),
                 jax.ShapeDtypeStruct((GATE_ROWS, t), F32),
                 jax.ShapeDtypeStruct((M_WIDTH, t), BF16), jax.ShapeDtypeStruct((M_WIDTH, t), BF16))
    return pl.pallas_call(
        functools.partial(_inproj_kernel, tm=tm, tiles_per_seq=s // tm),
        grid=(t // tm,),
        in_specs=[row(D_MODEL),
                  pl.BlockSpec((SUBLANES, D_MODEL), lambda i: (jnp.maximum(i * hb - 1, 0), 0)),
                  pl.BlockSpec((SUBLANES, D_MODEL), lambda i: (jnp.minimum((i + 1) * hb, n_halo - 1), 0)),
                  _resident((1, D_MODEL)), _resident((D_MODEL, N_MAIN)), _resident((GATE_ROWS, D_MODEL)),
                  _resident((GATE_ROWS, 1)), _resident((CONV_WIDTH, 2 * M_WIDTH)), _resident((1, 2 * M_WIDTH))],
        out_specs=out_specs,
        out_shape=out_shape,
        scratch_shapes=[pltpu.VMEM((D_MODEL // LANES, tm, LANES), F32), pltpu.VMEM((ATTN_W // LANES, tm, LANES), F32)],
        compiler_params=_params("parallel"),
        name="inproj",
    )(x, x, x, p["norm_mix_pre"], p["w_main"], p["w_gate"], p["b_gate"], p["w_conv"], p["b_conv"])


ATTN_TILE = 128
ATTN_KEYS = ATTN_TILE + 2 * HALF_WIN
ATTN_BLOCK = 1024
PAIR_W = 2 * HEAD_DIM


def _attn_init(bias, vw, *, slopes, dilation):
    qi = lax.broadcasted_iota(jnp.int32, (ATTN_TILE, ATTN_KEYS), 0)
    kj = lax.broadcasted_iota(jnp.int32, (ATTN_TILE, ATTN_KEYS), 1)
    step = jnp.abs(kj - HALF_WIN - qi)
    dist = (step * dilation).astype(F32)
    band = jnp.where(step <= HALF_WIN, 0.0, NEG)
    for variant in range(4):
        mask = band
        if variant & 1:
            mask = jnp.where(kj < HALF_WIN, NEG, mask)
        if variant & 2:
            mask = jnp.where(kj >= ATTN_TILE + HALF_WIN, NEG, mask)
        for h in range(HEADS_PER_GROUP):
            bias[variant, h] = -float(slopes[h]) * dist + mask
    ones = jnp.ones((vw.shape[0], PAIR_W), BF16)
    for pair in range(HEADS_PER_GROUP // 2):
        vw[:, (2 * pair + 1) * PAIR_W:(2 * pair + 2) * PAIR_W] = ones


def _attn_kernel(q_ref, k_ref, kp_ref, kn_ref, v_ref, vp_ref, vn_ref, o_ref, lse_ref, kw, vw, bias,
                 *, slopes, dilation, tqb, nq):
    i = pl.program_id(2)

    @pl.when((pl.program_id(0) == 0) & (pl.program_id(1) == 0) & (i == 0))
    def _():
        _attn_init(bias, vw, slopes=slopes, dilation=dilation)

    kw[0:HALF_WIN, :] = kp_ref[...]
    kw[HALF_WIN:HALF_WIN + tqb, :] = k_ref[...]
    kw[HALF_WIN + tqb:, :] = kn_ref[...]
    for pair in range(HEADS_PER_GROUP // 2):
        src = slice(pair * PAIR_W, (pair + 1) * PAIR_W)
        dst = slice(2 * pair * PAIR_W, (2 * pair + 1) * PAIR_W)
        vw[0:HALF_WIN, dst] = vp_ref[:, src]
        vw[HALF_WIN:HALF_WIN + tqb, dst] = v_ref[:, src]
        vw[HALF_WIN + tqb:, dst] = vn_ref[:, src]

    lane = lax.broadcasted_iota(jnp.int32, (ATTN_TILE, PAIR_W), 1)
    low = lane < HEAD_DIM
    n_tiles = tqb // ATTN_TILE
    for t in range(n_tiles):
        variant = 0
        if t == 0:
            variant = variant + jnp.where(i == 0, 1, 0)
        if t == n_tiles - 1:
            variant = variant + jnp.where(i == nq - 1, 2, 0)
        rows = slice(t * ATTN_TILE, (t + 1) * ATTN_TILE)
        krows = slice(t * ATTN_TILE, t * ATTN_TILE + ATTN_KEYS)
        heads = range(HEADS_PER_GROUP)
        pcols = [slice((h // 2) * PAIR_W, (h // 2 + 1) * PAIR_W) for h in heads]
        scores = []
        for h in heads:
            q2 = q_ref[rows, pcols[h]]
            qh = jnp.where(low if h % 2 == 0 else ~low, q2, jnp.zeros_like(q2))
            scores.append(lax.dot_general(qh, kw[krows, pcols[h]], _NT, preferred_element_type=F32))
        probs, maxes = [], []
        for h in heads:
            s = scores[h] + bias[variant, h]
            m = jnp.max(s, axis=-1, keepdims=True)
            probs.append(jnp.exp(s - m).astype(BF16))
            maxes.append(m)
        pvs = [jnp.dot(probs[h], vw[krows, (h // 2) * 2 * PAIR_W:(h // 2 + 1) * 2 * PAIR_W],
                       preferred_element_type=F32) for h in heads]
        lse = jnp.zeros((ATTN_TILE, PAIR_W), F32)
        outs = []
        for h in heads:
            l = pvs[h][:, PAIR_W:]
            outs.append(pvs[h][:, :PAIR_W] * (1.0 / l))
            lse = jnp.where(lane == h, maxes[h] + jnp.log(l), lse)
        lse_ref[rows, :] = lse[:, 0:HEADS_PER_GROUP]
        for pair in range(HEADS_PER_GROUP // 2):
            o_ref[rows, pcols[2 * pair]] = jnp.where(low, outs[2 * pair], outs[2 * pair + 1]).astype(o_ref.dtype)


def _attention_group(qk, v, g, nb, s):
    dilation = DILATIONS[g]
    rows = qk.shape[0]
    sub_len = s // dilation
    tqb = min(ATTN_BLOCK, sub_len)
    nq = sub_len // tqb
    hb = tqb // HALF_WIN
    n_halo = rows // HALF_WIN

    def spec(size, row_fn, col_fn):
        return pl.BlockSpec((size, ATTN_W), lambda b, c, i: (row_fn(b * nq + i), col_fn(c)))

    cur = lambda r: r
    prv = lambda r: jnp.maximum(r * hb - 1, 0)
    nxt = lambda r: jnp.minimum((r + 1) * hb, n_halo - 1)
    qcol, kcol, vcol = (lambda c: 2 * c), (lambda c: 2 * c + 1), (lambda c: c)
    kern = functools.partial(_attn_kernel, slopes=tuple(_alibi_slopes()[g]), dilation=dilation, tqb=tqb, nq=nq)
    return pl.pallas_call(
        kern,
        grid=(nb, dilation, nq),
        in_specs=[spec(tqb, cur, qcol), spec(tqb, cur, kcol), spec(HALF_WIN, prv, kcol), spec(HALF_WIN, nxt, kcol),
                  spec(tqb, cur, vcol), spec(HALF_WIN, prv, vcol), spec(HALF_WIN, nxt, vcol)],
        out_specs=(spec(tqb, cur, vcol),
                   pl.BlockSpec((None, tqb, HEADS_PER_GROUP), lambda b, c, i: (c, b * nq + i, 0))),
        out_shape=(jax.ShapeDtypeStruct((rows, dilation * ATTN_W), BF16),
                   jax.ShapeDtypeStruct((dilation, rows, HEADS_PER_GROUP), F32)),
        scratch_shapes=[pltpu.VMEM((tqb + 2 * HALF_WIN, ATTN_W), BF16),
                        pltpu.VMEM((tqb + 2 * HALF_WIN, 2 * ATTN_W), BF16),
                        pltpu.VMEM((4, HEADS_PER_GROUP, ATTN_TILE, ATTN_KEYS), F32)],
        compiler_params=_params("arbitrary", "arbitrary", "arbitrary"),
        name=f"attn_g{g}",
    )(qk, qk, qk, qk, v, v, v)


MLSTM_CHUNKS = 16


def _mlstm_stage1a(q, k, li_row, lf_row, tri):
    return dict(qk=lax.dot_general(q, k, _NT, preferred_element_type=F32), li_row=li_row,
                cum_col=jnp.sum(jnp.where(tri, lf_row, 0.0), axis=1, keepdims=True),
                tot=jnp.sum(lf_row, axis=1, keepdims=True))


def _mlstm_stage1b(rec, tri, eye):
    cum_row = jnp.sum(jnp.where(eye, rec["cum_col"], 0.0), axis=0, keepdims=True)
    li_row = rec.pop("li_row")
    dmat = jnp.where(tri, rec["cum_col"] - cum_row + li_row, NEG)
    a = rec["tot"] - cum_row + li_row
    rec.update(dmat=dmat, a=a, m_loc=jnp.max(dmat, axis=1, keepdims=True), a_max=jnp.max(a, axis=1, keepdims=True))


def _mlstm_stage1c(rec, m_prev):
    m_inter = rec["cum_col"] + m_prev
    m_t = jnp.maximum(m_inter, rec.pop("m_loc"))
    m_new = jnp.maximum(rec["tot"] + m_prev, rec.pop("a_max"))
    rec.update(p=jnp.exp(rec.pop("dmat") - m_t), w_k=jnp.exp(rec.pop("a") - m_new),
               w_inter=jnp.exp(m_inter - m_t), floor=jnp.exp(-m_t), decay=jnp.exp(rec["tot"] + m_prev - m_new))
    return m_new


def _mlstm_stage2(rec, kt):
    sc = rec["qk"] * rec["p"]
    w_k = rec["w_k"]
    rec.update(sc=sc.astype(BF16), den=jnp.sum(sc, axis=1, keepdims=True),
               kwt=(kt.astype(F32) * w_k).astype(BF16),
               w_rows=jnp.broadcast_to(w_k, (SUBLANES, w_k.shape[1])).astype(BF16))


def _mlstm_stage3(rec, k, v, vt):
    rec.update(num=jnp.dot(rec["sc"], v, preferred_element_type=F32),
               upd=lax.dot_general(rec["kwt"], vt, _NT, preferred_element_type=F32),
               n_upd=jnp.dot(rec["w_rows"], k, preferred_element_type=F32)[0:1, :])


def _mlstm_stage4a(rec, state):
    c_prev, n_prev = state
    rec.update(c_in=c_prev.astype(BF16), n_in=n_prev)
    decay = rec.pop("decay")
    return decay * c_prev + rec.pop("upd"), decay * n_prev + rec.pop("n_upd")


def _mlstm_stage4b(rec, q):
    rec.update(qc=jnp.dot(q, rec.pop("c_in"), preferred_element_type=F32),
               qn=jnp.sum(q.astype(F32) * rec.pop("n_in"), axis=1, keepdims=True))


def _mlstm_stage4c(rec):
    w_inter = rec["w_inter"]
    den = rec["den"] + w_inter * rec["qn"]
    return (rec["num"] + w_inter * rec["qc"]) * (1.0 / jnp.maximum(jnp.abs(den), rec["floor"]))


def _mlstm_kernel(qf, kf, vf, ktf, vtf, gf, qb, kb, vb, ktb, vtb, gb, hf_ref, hb_ref, c_scr, n_scr, m_scr):
    @pl.when(pl.program_id(2) == 0)
    def _():
        c_scr[...] = jnp.zeros_like(c_scr)
        n_scr[...] = jnp.zeros_like(n_scr)
        m_scr[...] = jnp.zeros_like(m_scr)

    L = M_CHUNK
    ti = lax.broadcasted_iota(jnp.int32, (L, L), 0)
    si = lax.broadcasted_iota(jnp.int32, (L, L), 1)
    eye = ti == si
    f_rows = [slice(c * L, (c + 1) * L) for c in range(MLSTM_CHUNKS)]
    dirs = (((qf, kf, vf, ktf, vtf, gf), (0, 1), si <= ti, f_rows, hf_ref),
            ((qb, kb, vb, ktb, vtb, gb), (2, 3), si >= ti, f_rows[::-1], hb_ref))
    recs = [[], []]
    for c in range(MLSTM_CHUNKS):
        for d, ((q, k, _, _, _, g), (gi, gf_), tri, rows, _) in enumerate(dirs):
            r = rows[c]
            recs[d].append(_mlstm_stage1a(q[r, :], k[r, :], g[gi:gi + 1, r], g[gf_:gf_ + 1, r], tri))
    for c in range(MLSTM_CHUNKS):
        for d in range(2):
            _mlstm_stage1b(recs[d][c], dirs[d][2], eye)
    m = [m_scr[0], m_scr[1]]
    for c in range(MLSTM_CHUNKS):
        for d in range(2):
            m[d] = _mlstm_stage1c(recs[d][c], m[d])
    for c in range(MLSTM_CHUNKS):
        for d, ((_, _, _, kt, _, _), _, _, rows, _) in enumerate(dirs):
            _mlstm_stage2(recs[d][c], kt[:, rows[c]])
    for c in range(MLSTM_CHUNKS):
        for d, ((_, k, v, _, vt, _), _, _, rows, _) in enumerate(dirs):
            _mlstm_stage3(recs[d][c], k[rows[c], :], v[rows[c], :], vt[:, rows[c]])
    state = [(c_scr[0], n_scr[0]), (c_scr[1], n_scr[1])]
    for c in range(MLSTM_CHUNKS):
        for d in range(2):
            state[d] = _mlstm_stage4a(recs[d][c], state[d])
    for c in range(MLSTM_CHUNKS):
        for d, ((q, _, _, _, _, _), _, _, rows, _) in enumerate(dirs):
            _mlstm_stage4b(recs[d][c], q[rows[c], :])
    for c in range(MLSTM_CHUNKS):
        for d, (_, _, _, rows, h_ref) in enumerate(dirs):
            h_ref[rows[c], :] = _mlstm_stage4c(recs[d][c]).astype(h_ref.dtype)
    for d in range(2):
        c_scr[d] = state[d][0]
        n_scr[d] = state[d][1]
        m_scr[d] = m[d]


def _mlstm(qk, v, kt, vt, gates, nb, s):
    t = v.shape[0]
    rows = MLSTM_CHUNKS * M_CHUNK
    nr = s // rows
    fblk = lambda b, j: b * nr + j
    bblk = lambda b, j: b * nr + (nr - 1 - j)

    def specs(blk):
        return [pl.BlockSpec((rows, M_HEAD_DIM), lambda b, h, j: (blk(b, j), h)),
                pl.BlockSpec((rows, M_HEAD_DIM), lambda b, h, j: (blk(b, j), M_HEADS + h)),
                pl.BlockSpec((rows, M_HEAD_DIM), lambda b, h, j: (blk(b, j), h)),
                pl.BlockSpec((M_HEAD_DIM, rows), lambda b, h, j: (h, blk(b, j))),
                pl.BlockSpec((M_HEAD_DIM, rows), lambda b, h, j: (h, blk(b, j))),
                pl.BlockSpec((SUBLANES, rows), lambda b, h, j: (h, blk(b, j)))]

    out = lambda blk: pl.BlockSpec((rows, M_HEAD_DIM), lambda b, h, j: (blk(b, j), h))
    return pl.pallas_call(
        _mlstm_kernel,
        grid=(nb, M_HEADS, nr),
        in_specs=specs(fblk) + specs(bblk),
        out_specs=(out(fblk), out(bblk)),
        out_shape=(jax.ShapeDtypeStruct((t, M_WIDTH), BF16), jax.ShapeDtypeStruct((t, M_WIDTH), BF16)),
        scratch_shapes=[pltpu.VMEM((2, M_HEAD_DIM, M_HEAD_DIM), F32), pltpu.VMEM((2, 1, M_HEAD_DIM), F32),
                        pltpu.VMEM((2, 1, 1), F32)],
        compiler_params=_params("parallel", "parallel", "arbitrary"),
        name="mlstm",
    )(qk, qk, v, kt, vt, gates, qk, qk, v, kt, vt, gates)


LSE_PAD = 128


def _expand_matrix():
    r = np.arange(LSE_PAD)[:, None]
    c = np.arange(N_GROUPS * ATTN_W)[None, :]
    hit = (r < N_GROUPS * 2 * HEADS_PER_GROUP) & (c // ATTN_W == r // (2 * HEADS_PER_GROUP)) \
        & ((c % ATTN_W) // HEAD_DIM == r % HEADS_PER_GROUP)
    return jnp.asarray(hit, dtype=BF16)


def _mix_out_kernel(o0, o1, o2, l0, l1, l2, hf, hb, mo, bg, x_ref, wa, wm, wo, gn, gpost, expand, y_ref,
                    wpad, os1, os2, *, tm):
    hs = hf[...].astype(F32) + hb[...].astype(F32)
    parts = []
    for h in range(M_HEADS):
        p = hs[:, h * M_HEAD_DIM:(h + 1) * M_HEAD_DIM]
        parts.append(p * lax.rsqrt(jnp.mean(p * p, axis=-1, keepdims=True) + EPS))
    hn = jnp.concatenate(parts, axis=-1) * gn[...]
    ml = mo[...].astype(F32) * hn
    ml_proj = jnp.dot(ml.astype(BF16), wm[...], preferred_element_type=F32)
    lses = [l0[...], l1[...], l2[...]]
    mx = jnp.maximum(jnp.maximum(lses[0], lses[1]), lses[2])
    es = [jnp.exp(l - mx) for l in lses]
    inv_z = 1.0 / (es[0] + es[1] + es[2])
    wpad[...] = jnp.zeros_like(wpad)
    for g, e in enumerate(es):
        w = e * inv_z
        w_hi = w.astype(BF16).astype(F32)
        base = g * 2 * HEADS_PER_GROUP
        wpad[:, base:base + HEADS_PER_GROUP] = w_hi
        wpad[:, base + HEADS_PER_GROUP:base + 2 * HEADS_PER_GROUP] = w - w_hi
    wide = jnp.dot(wpad[...].astype(BF16), expand[...], preferred_element_type=F32)
    for o, osc, d in ((o1, os1, DILATIONS[1]), (o2, os2, DILATIONS[2])):
        n = tm // d
        for c in range(d):
            for j in range(ATTN_W // LANES):
                lo = c * ATTN_W + j * LANES
                osc[j, pl.ds(c, n, stride=d), :] = o[:, lo:lo + LANES].astype(F32)
    outs = [o0[...].astype(F32)] + [jnp.concatenate([osc[j] for j in range(ATTN_W // LANES)], axis=1)
                                    for osc in (os1, os2)]
    attn = (wide[:, :ATTN_W] * outs[0] + wide[:, ATTN_W:2 * ATTN_W] * outs[1] + wide[:, 2 * ATTN_W:] * outs[2])
    attn_proj = jnp.dot(attn.astype(BF16), wa[...], preferred_element_type=F32)
    gates = bg[...].astype(F32)
    merged = gates[:, :D_MODEL] * attn_proj + gates[:, D_MODEL:] * ml_proj
    mix = jnp.dot(merged.astype(BF16), wo[...], preferred_element_type=F32)
    y_ref[...] = x_ref[...] + _rms(mix, gpost[...])


def _mix_out(outs, lses, hf, hb, mo, bg, x, p, tm):
    t = x.shape[0]
    row = lambda w: pl.BlockSpec((tm, w), lambda i: (i, 0))
    view = lambda d: pl.BlockSpec((tm // d, d * ATTN_W), lambda i: (i, 0))
    return pl.pallas_call(
        functools.partial(_mix_out_kernel, tm=tm),
        grid=(t // tm,),
        in_specs=[row(ATTN_W), view(4), view(16),
                  row(HEADS_PER_GROUP), row(HEADS_PER_GROUP), row(HEADS_PER_GROUP),
                  row(M_WIDTH), row(M_WIDTH), row(M_WIDTH), row(2 * D_MODEL), row(D_MODEL),
                  _resident((ATTN_W, D_MODEL)), _resident((M_WIDTH, D_MODEL)), _resident((D_MODEL, D_MODEL)),
                  _resident((1, M_WIDTH)), _resident((1, D_MODEL)), _resident((LSE_PAD, N_GROUPS * ATTN_W))],
        out_specs=row(D_MODEL),
        out_shape=jax.ShapeDtypeStruct((t, D_MODEL), F32),
        scratch_shapes=[pltpu.VMEM((tm, LSE_PAD), F32),
                        pltpu.VMEM((ATTN_W // LANES, tm, LANES), F32),
                        pltpu.VMEM((ATTN_W // LANES, tm, LANES), F32)],
        compiler_params=_params("parallel"),
        name="mix_out",
    )(*outs, *lses, hf, hb, mo, bg, x, p["w_attn_proj"], p["w_mlstm_proj"], p["w_out"],
      p["g_mlstm_norm"], p["norm_mix_post"], _expand_matrix())


MXU_TILE = 256
FFN_SPLITS = (0, 6 * MXU_TILE, FFN_HIDDEN)
assert FFN_HIDDEN % MXU_TILE == 0


def _ffn_kernel(x_ref, gpre, win, wout, gpost, y_ref):
    x = x_ref[...]
    xb = _rms(x, gpre[...]).astype(BF16)
    acc = jnp.zeros(x.shape, F32)
    for lo, hi in zip(FFN_SPLITS[:-1], FFN_SPLITS[1:]):
        gate = jnp.dot(xb, win[:, lo:hi], preferred_element_type=F32)
        up = jnp.dot(xb, win[:, FFN_HIDDEN + lo:FFN_HIDDEN + hi], preferred_element_type=F32)
        act = (gate * jax.nn.sigmoid(gate) * up).astype(BF16)
        acc = acc + jnp.dot(act, wout[lo:hi, :], preferred_element_type=F32)
    y_ref[...] = x + _rms(acc, gpost[...])


def _ffn(x, p, tm):
    t = x.shape[0]
    row = pl.BlockSpec((tm, D_MODEL), lambda i: (i, 0))
    return pl.pallas_call(
        _ffn_kernel,
        grid=(t // tm,),
        in_specs=[row, _resident((1, D_MODEL)), _resident((D_MODEL, 2 * FFN_HIDDEN)),
                  _resident((FFN_HIDDEN, D_MODEL)), _resident((1, D_MODEL))],
        out_specs=row,
        out_shape=jax.ShapeDtypeStruct((t, D_MODEL), F32),
        compiler_params=_params("parallel"),
        name="ffn",
    )(x, p["norm_ffn_pre"], p["w_ffn_in"], p["w_ffn_out"], p["norm_ffn_post"])


TM_PROJ = 256
TM_MIX = 512
TM_FFN = 1024


def _layer(x, nb, s, p):
    qk0, qk1, qk2, v0, v1, v2, mqk, mv, mo, bg, gates, mkt, mvt = _inproj(x, p, s, TM_PROJ)
    attn = [_attention_group(qk, v, g, nb, s) for g, (qk, v) in enumerate(((qk0, v0), (qk1, v1), (qk2, v2)))]
    hf, hb = _mlstm(mqk, mv, mkt, mvt, gates, nb, s)
    lses = [a[1].transpose(1, 0, 2).reshape(x.shape[0], HEADS_PER_GROUP) for a in attn]
    x = _mix_out([a[0] for a in attn], lses, hf, hb, mo, bg, x, p, TM_MIX)
    return _ffn(x, p, TM_FFN)


def _prepare(norm_mix_pre, norm_mix_post, norm_ffn_pre, norm_ffn_post, w_in, b_mlstm_gates, w_conv, b_conv,
             g_mlstm_norm, w_attn_proj, w_mlstm_proj, w_out, w_ffn_in, w_ffn_out):
    depth = w_in.shape[0]
    sizes = (3 * ATTN_W, 3 * ATTN_W, ATTN_W, 2 * M_WIDTH, M_WIDTH, M_WIDTH, 4 * M_HEADS, 2 * D_MODEL)
    offs = [int(c) for c in np.cumsum((0,) + sizes)]
    aq, ak, av, mqk, mv, mo, mg, bg = (w_in[:, :, offs[j]:offs[j + 1]] for j in range(8))
    aq = aq * (HEAD_DIM ** -0.5)
    qk = [jnp.concatenate([aq[:, :, g * ATTN_W:(g + 1) * ATTN_W], ak[:, :, g * ATTN_W:(g + 1) * ATTN_W]], axis=-1)
          for g in range(N_GROUPS)]
    w_main = jnp.concatenate([piece.astype(BF16) for piece in qk + [av, mqk, mv, mo, bg]], axis=-1)
    wg = mg.reshape(depth, D_MODEL, 4, M_HEADS).transpose(0, 3, 2, 1)
    wg = jnp.pad(wg, ((0, 0), (0, 0), (0, SUBLANES - 4), (0, 0))).reshape(depth, GATE_ROWS, D_MODEL).astype(BF16)
    bgate = b_mlstm_gates.reshape(depth, 4, M_HEADS).transpose(0, 2, 1)
    bgate = jnp.pad(bgate, ((0, 0), (0, 0), (0, SUBLANES - 4))).reshape(depth, GATE_ROWS, 1)
    layers = []
    for l in range(depth):
        layers.append(dict(
            norm_mix_pre=norm_mix_pre[l][None], norm_mix_post=norm_mix_post[l][None],
            norm_ffn_pre=norm_ffn_pre[l][None], norm_ffn_post=norm_ffn_post[l][None],
            w_main=w_main[l], w_gate=wg[l], b_gate=bgate[l],
            w_conv=w_conv[l], b_conv=b_conv[l][None], g_mlstm_norm=g_mlstm_norm[l][None],
            w_attn_proj=w_attn_proj[l].astype(BF16), w_mlstm_proj=w_mlstm_proj[l].astype(BF16),
            w_out=w_out[l].astype(BF16), w_ffn_in=w_ffn_in[l].astype(BF16), w_ffn_out=w_ffn_out[l].astype(BF16)))
    return layers


def _trunk(x, layers):
    nb, s, d = x.shape
    y = x.reshape(nb * s, d)
    for p in layers:
        y = _layer(y, nb, s, p)
    return y.reshape(nb, s, d)


def kernel(x_prompt, x_sample, norm_mix_pre, norm_mix_post, norm_ffn_pre, norm_ffn_post, w_in, b_mlstm_gates,
           w_conv, b_conv, g_mlstm_norm, w_attn_proj, w_mlstm_proj, w_out, w_ffn_in, w_ffn_out):
    layers = _prepare(norm_mix_pre, norm_mix_post, norm_ffn_pre, norm_ffn_post, w_in, b_mlstm_gates, w_conv,
                      b_conv, g_mlstm_norm, w_attn_proj, w_mlstm_proj, w_out, w_ffn_in, w_ffn_out)
    return _trunk(x_prompt, layers), _trunk(x_sample, layers)
```

```python
import functools

import numpy as np
import jax
import jax.numpy as jnp
from jax import lax
from jax.experimental import pallas as pl
from jax.experimental.pallas import tpu as pltpu

F32 = jnp.float32
BF16 = jnp.bfloat16

D_MODEL = 1024
ATTN_GROUPS = ((128, 1), (512, 4), (2048, 16))
DILATIONS = tuple(d for _, d in ATTN_GROUPS)
N_GROUPS = 3
HEADS_PER_GROUP = 8
HEAD_DIM = 64
HALF_WIN = 64
ATTN_W = HEADS_PER_GROUP * HEAD_DIM
M_WIDTH = 1024
M_HEADS = 4
M_HEAD_DIM = 256
M_CHUNK = 128
CONV_WIDTH = 5
FFN_HIDDEN = 2816
EPS = 1e-6
NEG = -1e30
SUBLANES = 8
GATE_ROWS = M_HEADS * SUBLANES

OFF_QK = 0
OFF_V = OFF_QK + 2 * N_GROUPS * ATTN_W
OFF_MQK = OFF_V + ATTN_W
OFF_MV = OFF_MQK + 2 * M_WIDTH
OFF_MO = OFF_MV + M_WIDTH
OFF_BG = OFF_MO + M_WIDTH
N_MAIN = OFF_BG + 2 * D_MODEL

VMEM_LIMIT = 56 * 1024 * 1024


def _alibi_slopes():
    h = np.arange(1, N_GROUPS * HEADS_PER_GROUP + 1, dtype=np.float32)
    return (2.0 ** (-8.0 * h / (N_GROUPS * HEADS_PER_GROUP))).astype(np.float32).reshape(N_GROUPS, HEADS_PER_GROUP)


def _rms(x, gain):
    return x * lax.rsqrt(jnp.mean(x * x, axis=-1, keepdims=True) + EPS) * gain


def _resident(shape):
    nd = len(shape)
    return pl.BlockSpec(shape, lambda *_: (0,) * nd, pipeline_mode=pl.Buffered(1))


_NT = (((1,), (1,)), ((), ()))
LANES = 128


def _slab_store(ref, row0, value):
    rows = value.shape[0]
    for j in range(ref.shape[0]):
        ref[j, row0:row0 + rows, :] = value[:, j * LANES:(j + 1) * LANES]


def _slab_rows(ref, start, size, stride):
    return jnp.concatenate([ref[j, pl.ds(start, size, stride=stride), :] for j in range(ref.shape[0])], axis=1)


def _params(*sem):
    return pltpu.CompilerParams(dimension_semantics=sem, vmem_limit_bytes=VMEM_LIMIT)


def _inproj_kernel(x_ref, xp_ref, xn_ref, gain_ref, w_ref, wg_ref, bias_ref, wc_ref, bc_ref,
                   qk0, qk1, qk2, v0, v1, v2, mqk, mv, mo, bg, gates, mkt, mvt, xs, vs, *, tm, tiles_per_seq):
    i = pl.program_id(0)
    gain = gain_ref[...]
    first = (i % tiles_per_seq) == 0
    last = (i % tiles_per_seq) == tiles_per_seq - 1
    xn = _rms(x_ref[...], gain)
    x_prev = jnp.where(first, 0.0, _rms(xp_ref[...], gain))
    x_next = jnp.where(last, 0.0, _rms(xn_ref[...], gain))
    _slab_store(xs, 0, xn)
    xb = xn.astype(BF16)
    xe = jnp.concatenate([x_prev, xn, x_next], axis=0).astype(BF16)
    ext = tm + 2 * SUBLANES
    cw = ATTN_W

    def proj(lhs, a, b):
        return jnp.dot(lhs, w_ref[:, a:b], preferred_element_type=F32)

    def conv_proj(cb):
        return proj(xe, OFF_MQK + cb * cw, OFF_MQK + (cb + 1) * cw)

    def conv_finish(cb, pm):
        acc = bc_ref[:, cb * cw:(cb + 1) * cw]
        for tap in range(CONV_WIDTH):
            shift = (CONV_WIDTH // 2 - tap) % ext
            shifted = pm if shift == 0 else pltpu.roll(pm, shift, 0)
            acc = acc + shifted[SUBLANES:SUBLANES + tm, :] * wc_ref[tap:tap + 1, cb * cw:(cb + 1) * cw]
        y = acc * jax.nn.sigmoid(acc)
        if cb < M_WIDTH // cw:
            y = y * (M_HEAD_DIM ** -0.5)
        mqk[:, cb * cw:(cb + 1) * cw] = y.astype(BF16)
        if cb >= M_WIDTH // cw:
            r0 = cb * cw - M_WIDTH
            mkt[r0:r0 + cw, :] = y.T.astype(BF16)

    def permuted_qk(g, ref):
        d = DILATIONS[g]
        n = tm // d
        xperm = jnp.concatenate([_slab_rows(xs, c, n, d) for c in range(d)], axis=0)
        pg = proj(xperm.astype(BF16), OFF_QK + g * 2 * ATTN_W, OFF_QK + (g + 1) * 2 * ATTN_W).astype(BF16)
        for c in range(d):
            ref[:, c * 2 * ATTN_W:(c + 1) * 2 * ATTN_W] = pg[c * n:(c + 1) * n, :]

    pm = conv_proj(0)
    qk0[...] = proj(xb, OFF_QK, OFF_QK + 2 * ATTN_W).astype(BF16)
    conv_finish(0, pm)
    pm = conv_proj(1)
    permuted_qk(1, qk1)
    conv_finish(1, pm)
    pm = conv_proj(2)
    permuted_qk(2, qk2)
    conv_finish(2, pm)
    pm = conv_proj(3)
    vf = proj(xb, OFF_V, OFF_V + ATTN_W)
    v0[...] = vf.astype(BF16)
    _slab_store(vs, 0, vf)
    for g, ref in ((1, v1), (2, v2)):
        d = DILATIONS[g]
        for c in range(d):
            ref[:, c * ATTN_W:(c + 1) * ATTN_W] = _slab_rows(vs, c, tm // d, d).astype(BF16)
    vm = proj(xb, OFF_MV, OFF_MV + M_WIDTH)
    mv[...] = vm.astype(BF16)
    mvt[...] = vm.T.astype(BF16)
    conv_finish(3, pm)
    mo[...] = jax.nn.sigmoid(proj(xb, OFF_MO, OFF_MO + M_WIDTH)).astype(BF16)
    bg[...] = jax.nn.sigmoid(proj(xb, OFF_BG, OFF_BG + 2 * D_MODEL)).astype(BF16)
    gp = lax.dot_general(wg_ref[...], xb, _NT, preferred_element_type=F32) + bias_ref[...]
    row = lax.broadcasted_iota(jnp.int32, gp.shape, 0)
    log_sig = jnp.minimum(gp, 0.0) - jnp.log1p(jnp.exp(-jnp.abs(gp)))
    gates[...] = jnp.where(row % 2 == 1, log_sig, gp)


def _inproj(x, p, s, tm):
    t = x.shape[0]
    hb = tm // SUBLANES
    n_halo = t // SUBLANES
    row = lambda w: pl.BlockSpec((tm, w), lambda i: (i, 0))
    view = lambda d, w: pl.BlockSpec((tm // d, d * w), lambda i: (i, 0))
    out_specs = (row(2 * ATTN_W), view(4, 2 * ATTN_W), view(16, 2 * ATTN_W),
                 row(ATTN_W), view(4, ATTN_W), view(16, ATTN_W),
                 row(2 * M_WIDTH), row(M_WIDTH), row(M_WIDTH), row(2 * D_MODEL),
                 pl.BlockSpec((GATE_ROWS, tm), lambda i: (0, i)),
                 pl.BlockSpec((M_WIDTH, tm), lambda i: (0, i)), pl.BlockSpec((M_WIDTH, tm), lambda i: (0, i)))
    bshape = lambda d, w: jax.ShapeDtypeStruct((t // d, d * w), BF16)
    out_shape = (bshape(1, 2 * ATTN_W), bshape(4, 2 * ATTN_W), bshape(16, 2 * ATTN_W),
                 bshape(1, ATTN_W), bshape(4, ATTN_W), bshape(16, ATTN_W),
                 bshape(1, 2 * M_WIDTH), bshape(1, M_WIDTH), bshape(1, M_WIDTH), bshape(1, 2 * D_MODEL),
                 jax.ShapeDtypeStruct((GATE_ROWS, t), F32),
                 jax.ShapeDtypeStruct((M_WIDTH, t), BF16), jax.ShapeDtypeStruct((M_WIDTH, t), BF16))
    return pl.pallas_call(
        functools.partial(_inproj_kernel, tm=tm, tiles_per_seq=s // tm),
        grid=(t // tm,),
        in_specs=[row(D_MODEL),
                  pl.BlockSpec((SUBLANES, D_MODEL), lambda i: (jnp.maximum(i * hb - 1, 0), 0)),
                  pl.BlockSpec((SUBLANES, D_MODEL), lambda i: (jnp.minimum((i + 1) * hb, n_halo - 1), 0)),
                  _resident((1, D_MODEL)), _resident((D_MODEL, N_MAIN)), _resident((GATE_ROWS, D_MODEL)),
                  _resident((GATE_ROWS, 1)), _resident((CONV_WIDTH, 2 * M_WIDTH)), _resident((1, 2 * M_WIDTH))],
        out_specs=out_specs,
        out_shape=out_shape,
        scratch_shapes=[pltpu.VMEM((D_MODEL // LANES, tm, LANES), F32), pltpu.VMEM((ATTN_W // LANES, tm, LANES), F32)],
        compiler_params=_params("parallel"),
        name="inproj",
    )(x, x, x, p["norm_mix_pre"], p["w_main"], p["w_gate"], p["b_gate"], p["w_conv"], p["b_conv"])


ATTN_TILE = 128
ATTN_KEYS = ATTN_TILE + 2 * HALF_WIN
ATTN_BLOCK = 1024
PAIR_W = 2 * HEAD_DIM


def _attn_init(bias, vw, *, slopes, dilation):
    qi = lax.broadcasted_iota(jnp.int32, (ATTN_TILE, ATTN_KEYS), 0)
    kj = lax.broadcasted_iota(jnp.int32, (ATTN_TILE, ATTN_KEYS), 1)
    step = jnp.abs(kj - HALF_WIN - qi)
    dist = (step * dilation).astype(F32)
    band = jnp.where(step <= HALF_WIN, 0.0, NEG)
    for variant in range(4):
        mask = band
        if variant & 1:
            mask = jnp.where(kj < HALF_WIN, NEG, mask)
        if variant & 2:
            mask = jnp.where(kj >= ATTN_TILE + HALF_WIN, NEG, mask)
        for h in range(HEADS_PER_GROUP):
            bias[variant, h] = -float(slopes[h]) * dist + mask
    ones = jnp.ones((vw.shape[0], PAIR_W), BF16)
    for pair in range(HEADS_PER_GROUP // 2):
        vw[:, (2 * pair + 1) * PAIR_W:(2 * pair + 2) * PAIR_W] = ones


def _attn_kernel(q_ref, k_ref, kp_ref, kn_ref, v_ref, vp_ref, vn_ref, o_ref, lse_ref, kw, vw, bias,
                 *, slopes, dilation, tqb, nq):
    i = pl.program_id(2)

    @pl.when((pl.program_id(0) == 0) & (pl.program_id(1) == 0) & (i == 0))
    def _():
        _attn_init(bias, vw, slopes=slopes, dilation=dilation)

    kw[0:HALF_WIN, :] = kp_ref[...]
    kw[HALF_WIN:HALF_WIN + tqb, :] = k_ref[...]
    kw[HALF_WIN + tqb:, :] = kn_ref[...]
    for pair in range(HEADS_PER_GROUP // 2):
        src = slice(pair * PAIR_W, (pair + 1) * PAIR_W)
        dst = slice(2 * pair * PAIR_W, (2 * pair + 1) * PAIR_W)
        vw[0:HALF_WIN, dst] = vp_ref[:, src]
        vw[HALF_WIN:HALF_WIN + tqb, dst] = v_ref[:, src]
        vw[HALF_WIN + tqb:, dst] = vn_ref[:, src]

    lane = lax.broadcasted_iota(jnp.int32, (ATTN_TILE, PAIR_W), 1)
    low = lane < HEAD_DIM
    n_tiles = tqb // ATTN_TILE
    for t in range(n_tiles):
        variant = 0
        if t == 0:
            variant = variant + jnp.where(i == 0, 1, 0)
        if t == n_tiles - 1:
            variant = variant + jnp.where(i == nq - 1, 2, 0)
        rows = slice(t * ATTN_TILE, (t + 1) * ATTN_TILE)
        krows = slice(t * ATTN_TILE, t * ATTN_TILE + ATTN_KEYS)
        heads = range(HEADS_PER_GROUP)
        pcols = [slice((h // 2) * PAIR_W, (h // 2 + 1) * PAIR_W) for h in heads]
        scores = []
        for h in heads:
            q2 = q_ref[rows, pcols[h]]
            qh = jnp.where(low if h % 2 == 0 else ~low, q2, jnp.zeros_like(q2))
            scores.append(lax.dot_general(qh, kw[krows, pcols[h]], _NT, preferred_element_type=F32))
        probs, maxes = [], []
        for h in heads:
            s = scores[h] + bias[variant, h]
            m = jnp.max(s, axis=-1, keepdims=True)
            probs.append(jnp.exp(s - m).astype(BF16))
            maxes.append(m)
        pvs = [jnp.dot(probs[h], vw[krows, (h // 2) * 2 * PAIR_W:(h // 2 + 1) * 2 * PAIR_W],
                       preferred_element_type=F32) for h in heads]
        lse = jnp.zeros((ATTN_TILE, PAIR_W), F32)
        outs = []
        for h in heads:
            l = pvs[h][:, PAIR_W:]
            outs.append(pvs[h][:, :PAIR_W] * (1.0 / l))
            lse = jnp.where(lane == h, maxes[h] + jnp.log(l), lse)
        lse_ref[rows, :] = lse
        for pair in range(HEADS_PER_GROUP // 2):
            o_ref[rows, pcols[2 * pair]] = jnp.where(low, outs[2 * pair], outs[2 * pair + 1]).astype(o_ref.dtype)


def _attention_group(qk, v, g, nb, s):
    dilation = DILATIONS[g]
    rows = qk.shape[0]
    sub_len = s // dilation
    tqb = min(ATTN_BLOCK, sub_len)
    nq = sub_len // tqb
    hb = tqb // HALF_WIN
    n_halo = rows // HALF_WIN

    def spec(size, row_fn, col_fn):
        return pl.BlockSpec((size, ATTN_W), lambda b, c, i: (row_fn(b * nq + i), col_fn(c)))

    cur = lambda r: r
    prv = lambda r: jnp.maximum(r * hb - 1, 0)
    nxt = lambda r: jnp.minimum((r + 1) * hb, n_halo - 1)
    qcol, kcol, vcol = (lambda c: 2 * c), (lambda c: 2 * c + 1), (lambda c: c)
    kern = functools.partial(_attn_kernel, slopes=tuple(_alibi_slopes()[g]), dilation=dilation, tqb=tqb, nq=nq)
    return pl.pallas_call(
        kern,
        grid=(nb, dilation, nq),
        in_specs=[spec(tqb, cur, qcol), spec(tqb, cur, kcol), spec(HALF_WIN, prv, kcol), spec(HALF_WIN, nxt, kcol),
                  spec(tqb, cur, vcol), spec(HALF_WIN, prv, vcol), spec(HALF_WIN, nxt, vcol)],
        out_specs=(spec(tqb, cur, vcol),
                   pl.BlockSpec((tqb, PAIR_W), lambda b, c, i: (b * nq + i, c))),
        out_shape=(jax.ShapeDtypeStruct((rows, dilation * ATTN_W), BF16),
                   jax.ShapeDtypeStruct((rows, dilation * PAIR_W), F32)),
        scratch_shapes=[pltpu.VMEM((tqb + 2 * HALF_WIN, ATTN_W), BF16),
                        pltpu.VMEM((tqb + 2 * HALF_WIN, 2 * ATTN_W), BF16),
                        pltpu.VMEM((4, HEADS_PER_GROUP, ATTN_TILE, ATTN_KEYS), F32)],
        compiler_params=_params("arbitrary", "arbitrary", "arbitrary"),
        name=f"attn_g{g}",
    )(qk, qk, qk, qk, v, v, v)


MLSTM_CHUNKS = 16


def _mlstm_stage1a(q, k, li_row, lf_row, tri):
    return dict(qk=lax.dot_general(q, k, _NT, preferred_element_type=F32), li_row=li_row,
                cum_col=jnp.sum(jnp.where(tri, lf_row, 0.0), axis=1, keepdims=True),
                tot=jnp.sum(lf_row, axis=1, keepdims=True))


def _mlstm_stage1b(rec, tri, eye):
    cum_row = jnp.sum(jnp.where(eye, rec["cum_col"], 0.0), axis=0, keepdims=True)
    li_row = rec.pop("li_row")
    dmat = jnp.where(tri, rec["cum_col"] - cum_row + li_row, NEG)
    a = rec["tot"] - cum_row + li_row
    rec.update(dmat=dmat, a=a, m_loc=jnp.max(dmat, axis=1, keepdims=True), a_max=jnp.max(a, axis=1, keepdims=True))


def _mlstm_stage1c(rec, m_prev):
    m_inter = rec["cum_col"] + m_prev
    m_t = jnp.maximum(m_inter, rec.pop("m_loc"))
    m_new = jnp.maximum(rec["tot"] + m_prev, rec.pop("a_max"))
    rec.update(p=jnp.exp(rec.pop("dmat") - m_t), w_k=jnp.exp(rec.pop("a") - m_new),
               w_inter=jnp.exp(m_inter - m_t), floor=jnp.exp(-m_t), decay=jnp.exp(rec["tot"] + m_prev - m_new))
    return m_new


def _mlstm_stage2(rec, kt):
    sc = rec["qk"] * rec["p"]
    w_k = rec["w_k"]
    rec.update(sc=sc.astype(BF16), den=jnp.sum(sc, axis=1, keepdims=True),
               kwt=(kt.astype(F32) * w_k).astype(BF16),
               w_rows=jnp.broadcast_to(w_k, (SUBLANES, w_k.shape[1])).astype(BF16))


def _mlstm_stage3(rec, k, v, vt):
    rec.update(num=jnp.dot(rec["sc"], v, preferred_element_type=F32),
               upd=lax.dot_general(rec["kwt"], vt, _NT, preferred_element_type=F32),
               n_upd=jnp.dot(rec["w_rows"], k, preferred_element_type=F32)[0:1, :])


def _mlstm_stage4a(rec, state):
    c_prev, n_prev = state
    rec.update(c_in=c_prev.astype(BF16), n_in=n_prev)
    decay = rec.pop("decay")
    return decay * c_prev + rec.pop("upd"), decay * n_prev + rec.pop("n_upd")


def _mlstm_stage4b(rec, q):
    rec.update(qc=jnp.dot(q, rec.pop("c_in"), preferred_element_type=F32),
               qn=jnp.sum(q.astype(F32) * rec.pop("n_in"), axis=1, keepdims=True))


def _mlstm_stage4c(rec):
    w_inter = rec["w_inter"]
    den = rec["den"] + w_inter * rec["qn"]
    return (rec["num"] + w_inter * rec["qc"]) * (1.0 / jnp.maximum(jnp.abs(den), rec["floor"]))


def _mlstm_kernel(qf, kf, vf, ktf, vtf, gf, qb, kb, vb, ktb, vtb, gb, hf_ref, hb_ref, c_scr, n_scr, m_scr):
    @pl.when(pl.program_id(2) == 0)
    def _():
        c_scr[...] = jnp.zeros_like(c_scr)
        n_scr[...] = jnp.zeros_like(n_scr)
        m_scr[...] = jnp.zeros_like(m_scr)

    L = M_CHUNK
    ti = lax.broadcasted_iota(jnp.int32, (L, L), 0)
    si = lax.broadcasted_iota(jnp.int32, (L, L), 1)
    eye = ti == si
    f_rows = [slice(c * L, (c + 1) * L) for c in range(MLSTM_CHUNKS)]
    dirs = (((qf, kf, vf, ktf, vtf, gf), (0, 1), si <= ti, f_rows, hf_ref),
            ((qb, kb, vb, ktb, vtb, gb), (2, 3), si >= ti, f_rows[::-1], hb_ref))
    recs = [[], []]
    for c in range(MLSTM_CHUNKS):
        for d, ((q, k, _, _, _, g), (gi, gf_), tri, rows, _) in enumerate(dirs):
            r = rows[c]
            recs[d].append(_mlstm_stage1a(q[r, :], k[r, :], g[gi:gi + 1, r], g[gf_:gf_ + 1, r], tri))
    for c in range(MLSTM_CHUNKS):
        for d in range(2):
            _mlstm_stage1b(recs[d][c], dirs[d][2], eye)
    m = [m_scr[0], m_scr[1]]
    for c in range(MLSTM_CHUNKS):
        for d in range(2):
            m[d] = _mlstm_stage1c(recs[d][c], m[d])
    for c in range(MLSTM_CHUNKS):
        for d, ((_, _, _, kt, _, _), _, _, rows, _) in enumerate(dirs):
            _mlstm_stage2(recs[d][c], kt[:, rows[c]])
    for c in range(MLSTM_CHUNKS):
        for d, ((_, k, v, _, vt, _), _, _, rows, _) in enumerate(dirs):
            _mlstm_stage3(recs[d][c], k[rows[c], :], v[rows[c], :], vt[:, rows[c]])
    state = [(c_scr[0], n_scr[0]), (c_scr[1], n_scr[1])]
    for c in range(MLSTM_CHUNKS):
        for d in range(2):
            state[d] = _mlstm_stage4a(recs[d][c], state[d])
    for c in range(MLSTM_CHUNKS):
        for d, ((q, _, _, _, _, _), _, _, rows, _) in enumerate(dirs):
            _mlstm_stage4b(recs[d][c], q[rows[c], :])
    for c in range(MLSTM_CHUNKS):
        for d, (_, _, _, rows, h_ref) in enumerate(dirs):
            h_ref[rows[c], :] = _mlstm_stage4c(recs[d][c]).astype(h_ref.dtype)
    for d in range(2):
        c_scr[d] = state[d][0]
        n_scr[d] = state[d][1]
        m_scr[d] = m[d]


def _mlstm(qk, v, kt, vt, gates, nb, s):
    t = v.shape[0]
    rows = MLSTM_CHUNKS * M_CHUNK
    nr = s // rows
    fblk = lambda b, j: b * nr + j
    bblk = lambda b, j: b * nr + (nr - 1 - j)

    def specs(blk):
        return [pl.BlockSpec((rows, M_HEAD_DIM), lambda b, h, j: (blk(b, j), h)),
                pl.BlockSpec((rows, M_HEAD_DIM), lambda b, h, j: (blk(b, j), M_HEADS + h)),
                pl.BlockSpec((rows, M_HEAD_DIM), lambda b, h, j: (blk(b, j), h)),
                pl.BlockSpec((M_HEAD_DIM, rows), lambda b, h, j: (h, blk(b, j))),
                pl.BlockSpec((M_HEAD_DIM, rows), lambda b, h, j: (h, blk(b, j))),
                pl.BlockSpec((SUBLANES, rows), lambda b, h, j: (h, blk(b, j)))]

    out = lambda blk: pl.BlockSpec((rows, M_HEAD_DIM), lambda b, h, j: (blk(b, j), h))
    return pl.pallas_call(
        _mlstm_kernel,
        grid=(nb, M_HEADS, nr),
        in_specs=specs(fblk) + specs(bblk),
        out_specs=(out(fblk), out(bblk)),
        out_shape=(jax.ShapeDtypeStruct((t, M_WIDTH), BF16), jax.ShapeDtypeStruct((t, M_WIDTH), BF16)),
        scratch_shapes=[pltpu.VMEM((2, M_HEAD_DIM, M_HEAD_DIM), F32), pltpu.VMEM((2, 1, M_HEAD_DIM), F32),
                        pltpu.VMEM((2, 1, 1), F32)],
        compiler_params=_params("parallel", "parallel", "arbitrary"),
        name="mlstm",
    )(qk, qk, v, kt, vt, gates, qk, qk, v, kt, vt, gates)


LSE_PAD = 128


def _expand_matrix():
    r = np.arange(LSE_PAD)[:, None]
    c = np.arange(N_GROUPS * ATTN_W)[None, :]
    hit = (r < N_GROUPS * 2 * HEADS_PER_GROUP) & (c // ATTN_W == r // (2 * HEADS_PER_GROUP)) \
        & ((c % ATTN_W) // HEAD_DIM == r % HEADS_PER_GROUP)
    return jnp.asarray(hit, dtype=BF16)


def _mix_out_kernel(o0, o1, o2, l0, l1, l2, hf, hb, mo, bg, x_ref, wa, wm, wo, gn, gpost, expand, y_ref,
                    wpad, os1, os2, ls1, ls2, *, tm):
    hs = hf[...].astype(F32) + hb[...].astype(F32)
    parts = []
    for h in range(M_HEADS):
        p = hs[:, h * M_HEAD_DIM:(h + 1) * M_HEAD_DIM]
        parts.append(p * lax.rsqrt(jnp.mean(p * p, axis=-1, keepdims=True) + EPS))
    hn = jnp.concatenate(parts, axis=-1) * gn[...]
    ml = mo[...].astype(F32) * hn
    ml_proj = jnp.dot(ml.astype(BF16), wm[...], preferred_element_type=F32)
    for l, lsc, d in ((l1, ls1, DILATIONS[1]), (l2, ls2, DILATIONS[2])):
        for c in range(d):
            lsc[pl.ds(c, tm // d, stride=d), :] = l[:, c * LSE_PAD:(c + 1) * LSE_PAD]
    lses = [ref[:, 0:HEADS_PER_GROUP] for ref in (l0, ls1, ls2)]
    mx = jnp.maximum(jnp.maximum(lses[0], lses[1]), lses[2])
    es = [jnp.exp(l - mx) for l in lses]
    inv_z = 1.0 / (es[0] + es[1] + es[2])
    wpad[...] = jnp.zeros_like(wpad)
    for g, e in enumerate(es):
        w = e * inv_z
        w_hi = w.astype(BF16).astype(F32)
        base = g * 2 * HEADS_PER_GROUP
        wpad[:, base:base + HEADS_PER_GROUP] = w_hi
        wpad[:, base + HEADS_PER_GROUP:base + 2 * HEADS_PER_GROUP] = w - w_hi
    wide = jnp.dot(wpad[...].astype(BF16), expand[...], preferred_element_type=F32)
    for o, osc, d in ((o1, os1, DILATIONS[1]), (o2, os2, DILATIONS[2])):
        n = tm // d
        for c in range(d):
            for j in range(ATTN_W // LANES):
                lo = c * ATTN_W + j * LANES
                osc[j, pl.ds(c, n, stride=d), :] = o[:, lo:lo + LANES].astype(F32)
    outs = [o0[...].astype(F32)] + [jnp.concatenate([osc[j] for j in range(ATTN_W // LANES)], axis=1)
                                    for osc in (os1, os2)]
    attn = (wide[:, :ATTN_W] * outs[0] + wide[:, ATTN_W:2 * ATTN_W] * outs[1] + wide[:, 2 * ATTN_W:] * outs[2])
    attn_proj = jnp.dot(attn.astype(BF16), wa[...], preferred_element_type=F32)
    gates = bg[...].astype(F32)
    merged = gates[:, :D_MODEL] * attn_proj + gates[:, D_MODEL:] * ml_proj
    mix = jnp.dot(merged.astype(BF16), wo[...], preferred_element_type=F32)
    y_ref[...] = x_ref[...] + _rms(mix, gpost[...])


def _mix_out(outs, lses, hf, hb, mo, bg, x, p, tm):
    t = x.shape[0]
    row = lambda w: pl.BlockSpec((tm, w), lambda i: (i, 0))
    view = lambda d, w: pl.BlockSpec((tm // d, d * w), lambda i: (i, 0))
    return pl.pallas_call(
        functools.partial(_mix_out_kernel, tm=tm),
        grid=(t // tm,),
        in_specs=[row(ATTN_W), view(4, ATTN_W), view(16, ATTN_W),
                  row(LSE_PAD), view(4, LSE_PAD), view(16, LSE_PAD),
                  row(M_WIDTH), row(M_WIDTH), row(M_WIDTH), row(2 * D_MODEL), row(D_MODEL),
                  _resident((ATTN_W, D_MODEL)), _resident((M_WIDTH, D_MODEL)), _resident((D_MODEL, D_MODEL)),
                  _resident((1, M_WIDTH)), _resident((1, D_MODEL)), _resident((LSE_PAD, N_GROUPS * ATTN_W))],
        out_specs=row(D_MODEL),
        out_shape=jax.ShapeDtypeStruct((t, D_MODEL), F32),
        scratch_shapes=[pltpu.VMEM((tm, LSE_PAD), F32),
                        pltpu.VMEM((ATTN_W // LANES, tm, LANES), F32),
                        pltpu.VMEM((ATTN_W // LANES, tm, LANES), F32),
                        pltpu.VMEM((tm, LSE_PAD), F32), pltpu.VMEM((tm, LSE_PAD), F32)],
        compiler_params=_params("parallel"),
        name="mix_out",
    )(*outs, *lses, hf, hb, mo, bg, x, p["w_attn_proj"], p["w_mlstm_proj"], p["w_out"],
      p["g_mlstm_norm"], p["norm_mix_post"], _expand_matrix())


MXU_TILE = 256
FFN_SPLITS = (0, 6 * MXU_TILE, FFN_HIDDEN)
assert FFN_HIDDEN % MXU_TILE == 0


def _ffn_kernel(x_ref, gpre, win, wout, gpost, y_ref):
    x = x_ref[...]
    xb = _rms(x, gpre[...]).astype(BF16)
    acc = jnp.zeros(x.shape, F32)
    for lo, hi in zip(FFN_SPLITS[:-1], FFN_SPLITS[1:]):
        gate = jnp.dot(xb, win[:, lo:hi], preferred_element_type=F32)
        up = jnp.dot(xb, win[:, FFN_HIDDEN + lo:FFN_HIDDEN + hi], preferred_element_type=F32)
        act = (gate * jax.nn.sigmoid(gate) * up).astype(BF16)
        acc = acc + jnp.dot(act, wout[lo:hi, :], preferred_element_type=F32)
    y_ref[...] = x + _rms(acc, gpost[...])


def _ffn(x, p, tm):
    t = x.shape[0]
    row = pl.BlockSpec((tm, D_MODEL), lambda i: (i, 0))
    return pl.pallas_call(
        _ffn_kernel,
        grid=(t // tm,),
        in_specs=[row, _resident((1, D_MODEL)), _resident((D_MODEL, 2 * FFN_HIDDEN)),
                  _resident((FFN_HIDDEN, D_MODEL)), _resident((1, D_MODEL))],
        out_specs=row,
        out_shape=jax.ShapeDtypeStruct((t, D_MODEL), F32),
        compiler_params=_params("parallel"),
        name="ffn",
    )(x, p["norm_ffn_pre"], p["w_ffn_in"], p["w_ffn_out"], p["norm_ffn_post"])


TM_PROJ = 256
TM_MIX = 512
TM_FFN = 1024


def _layer(x, nb, s, p):
    qk0, qk1, qk2, v0, v1, v2, mqk, mv, mo, bg, gates, mkt, mvt = _inproj(x, p, s, TM_PROJ)
    attn = [_attention_group(qk, v, g, nb, s) for g, (qk, v) in enumerate(((qk0, v0), (qk1, v1), (qk2, v2)))]
    hf, hb = _mlstm(mqk, mv, mkt, mvt, gates, nb, s)
    x = _mix_out([a[0] for a in attn], [a[1] for a in attn], hf, hb, mo, bg, x, p, TM_MIX)
    return _ffn(x, p, TM_FFN)


def _prepare(norm_mix_pre, norm_mix_post, norm_ffn_pre, norm_ffn_post, w_in, b_mlstm_gates, w_conv, b_conv,
             g_mlstm_norm, w_attn_proj, w_mlstm_proj, w_out, w_ffn_in, w_ffn_out):
    depth = w_in.shape[0]
    sizes = (3 * ATTN_W, 3 * ATTN_W, ATTN_W, 2 * M_WIDTH, M_WIDTH, M_WIDTH, 4 * M_HEADS, 2 * D_MODEL)
    offs = [int(c) for c in np.cumsum((0,) + sizes)]
    aq, ak, av, mqk, mv, mo, mg, bg = (w_in[:, :, offs[j]:offs[j + 1]] for j in range(8))
    aq = aq * (HEAD_DIM ** -0.5)
    qk = [jnp.concatenate([aq[:, :, g * ATTN_W:(g + 1) * ATTN_W], ak[:, :, g * ATTN_W:(g + 1) * ATTN_W]], axis=-1)
          for g in range(N_GROUPS)]
    w_main = jnp.concatenate([piece.astype(BF16) for piece in qk + [av, mqk, mv, mo, bg]], axis=-1)
    wg = mg.reshape(depth, D_MODEL, 4, M_HEADS).transpose(0, 3, 2, 1)
    wg = jnp.pad(wg, ((0, 0), (0, 0), (0, SUBLANES - 4), (0, 0))).reshape(depth, GATE_ROWS, D_MODEL).astype(BF16)
    bgate = b_mlstm_gates.reshape(depth, 4, M_HEADS).transpose(0, 2, 1)
    bgate = jnp.pad(bgate, ((0, 0), (0, 0), (0, SUBLANES - 4))).reshape(depth, GATE_ROWS, 1)
    layers = []
    for l in range(depth):
        layers.append(dict(
            norm_mix_pre=norm_mix_pre[l][None], norm_mix_post=norm_mix_post[l][None],
            norm_ffn_pre=norm_ffn_pre[l][None], norm_ffn_post=norm_ffn_post[l][None],
            w_main=w_main[l], w_gate=wg[l], b_gate=bgate[l],
            w_conv=w_conv[l], b_conv=b_conv[l][None], g_mlstm_norm=g_mlstm_norm[l][None],
            w_attn_proj=w_attn_proj[l].astype(BF16), w_mlstm_proj=w_mlstm_proj[l].astype(BF16),
            w_out=w_out[l].astype(BF16), w_ffn_in=w_ffn_in[l].astype(BF16), w_ffn_out=w_ffn_out[l].astype(BF16)))
    return layers


def _trunk(x, layers):
    nb, s, d = x.shape
    y = x.reshape(nb * s, d)
    for p in layers:
        y = _layer(y, nb, s, p)
    return y.reshape(nb, s, d)


def kernel(x_prompt, x_sample, norm_mix_pre, norm_mix_post, norm_ffn_pre, norm_ffn_post, w_in, b_mlstm_gates,
           w_conv, b_conv, g_mlstm_norm, w_attn_proj, w_mlstm_proj, w_out, w_ffn_in, w_ffn_out):
    layers = _prepare(norm_mix_pre, norm_mix_post, norm_ffn_pre, norm_ffn_post, w_in, b_mlstm_gates, w_conv,
                      b_conv, g_mlstm_norm, w_attn_proj, w_mlstm_proj, w_out, w_ffn_in, w_ffn_out)
    return _trunk(x_prompt, layers), _trunk(x_sample, layers)
```
